```python
import math
import jax, jax.numpy as jnp
from jax import lax
import numpy as np

D_MODEL = 1024
BATCH = 2
SEQ = 8192
DEPTH = 2
DEC_BATCH = 128
DEC_SEQ = 8
PAST_LEN = 2048
PAGE_SIZE = 128

CONV_CH = D_MODEL // 2
N_CONV_GROUPS = 8
CONV_K = 3
N_HEADS = 4
QK_DIM = 64
V_DIM = 2 * QK_DIM
ATTN_W = N_HEADS * V_DIM
QK_W = N_HEADS * 2 * QK_DIM
MIX_W = CONV_CH + ATTN_W
D_IN = 3 * CONV_CH + 2 * QK_W + ATTN_W
SPLITS = (CONV_CH, 2 * CONV_CH, 3 * CONV_CH, 3 * CONV_CH + QK_W, 3 * CONV_CH + 2 * QK_W)
D_FF = 2816
NUM_BUCKETS = 32
MAX_DISTANCE = 128
Q_BLOCK = 128
LN_EPS = 1e-5
ATTN_SCALE = QK_DIM ** -0.5
ALPHA = (2 * DEPTH) ** 0.25
BETA = (8 * DEPTH) ** -0.25

kernel_name = "hymba_conv_diffattn_macaron_deepnorm_step"


def layer_norm(x, g, b):
    xf = x.astype(jnp.float32)
    mu = jnp.mean(xf, axis=-1, keepdims=True)
    var = jnp.mean(jnp.square(xf - mu), axis=-1, keepdims=True)
    return ((xf - mu) * lax.rsqrt(var + LN_EPS) * g + b).astype(x.dtype)


def swiglu(x, wg, wu, wd):
    return (jax.nn.silu(x @ wg) * (x @ wu)) @ wd


def rel_bucket(dist):
    n = jnp.maximum(dist, 0)
    max_exact = NUM_BUCKETS // 2
    nf = jnp.maximum(n, 1).astype(jnp.float32)
    large = max_exact + (jnp.log(nf / max_exact) / math.log(MAX_DISTANCE / max_exact)
                         * (NUM_BUCKETS - max_exact)).astype(jnp.int32)
    large = jnp.minimum(large, NUM_BUCKETS - 1)
    return jnp.where(n < max_exact, n, large)


def rel_bias(table, q_pos, k_pos):
    buckets = rel_bucket(q_pos[:, None] - k_pos[None, :])
    return jnp.moveaxis(table[buckets], -1, 0).astype(jnp.float32)


def lambda_full(lq1, lk1, lq2, lk2, lam_init):
    f32 = jnp.float32
    return (jnp.exp(jnp.sum(lq1.astype(f32) * lk1.astype(f32)))
            - jnp.exp(jnp.sum(lq2.astype(f32) * lk2.astype(f32))) + lam_init)


def diff_logits(q, k):
    return jnp.einsum('bqhcd,bkhcd->bhcqk', q, k).astype(jnp.float32) * ATTN_SCALE


def short_conv(u, prev, w):
    full = jnp.concatenate([prev, u], axis=1)
    t = u.shape[1]
    y = w[0] * full[:, 0:t]
    for j in range(1, CONV_K):
        y = y + w[j] * full[:, j:j + t]
    return y, full[:, -(CONV_K - 1):]


def prompt_attention(q, k, v, lam, table):
    b, s = q.shape[0], q.shape[1]
    nblk = s // Q_BLOCK
    qb = q.reshape(b, nblk, Q_BLOCK, N_HEADS, 2, QK_DIM).swapaxes(0, 1)
    k_pos = jnp.arange(s)

    def block(args):
        qi, bi = args
        q_pos = bi * Q_BLOCK + jnp.arange(Q_BLOCK)
        logits = diff_logits(qi, k) + rel_bias(table, q_pos, k_pos)[None, :, None]
        logits = jnp.where(k_pos[None, :] <= q_pos[:, None], logits, -jnp.inf)
        p = jax.nn.softmax(logits, axis=-1)
        a = (p[:, :, 0] - lam * p[:, :, 1]).astype(v.dtype)
        return jnp.einsum('bhqk,bkhd->bqhd', a, v)

    o = lax.map(block, (qb, jnp.arange(nblk)))
    return o.swapaxes(0, 1).reshape(b, s, N_HEADS, V_DIM)


def sample_attention(q, k_new, v_new, k_past, v_past, lam, table):
    t = q.shape[1]
    past = k_past.shape[1]
    pos_new = past + jnp.arange(t)
    lp = diff_logits(q, k_past) + rel_bias(table, pos_new, jnp.arange(past))[None, :, None]
    ln = diff_logits(q, k_new) + rel_bias(table, pos_new, pos_new)[None, :, None]
    ln = jnp.where(pos_new[None, :] <= pos_new[:, None], ln, -jnp.inf)
    p = jax.nn.softmax(jnp.concatenate([lp, ln], axis=-1), axis=-1)
    a = (p[:, :, 0] - lam * p[:, :, 1]).astype(v_new.dtype)
    return (jnp.einsum('bhqk,bkhd->bqhd', a[..., :past], v_past)
            + jnp.einsum('bhqk,bkhd->bqhd', a[..., past:], v_new))


def head_out(o, subln_w, lam_init, dtype):
    of = o.astype(jnp.float32)
    of = of * lax.rsqrt(jnp.mean(jnp.square(of), axis=-1, keepdims=True) + LN_EPS)
    of = of * subln_w * (1.0 - lam_init)
    return of.reshape(o.shape[0], o.shape[1], ATTN_W).astype(dtype)


def layer_forward(x, conv_prev, attn_core, lam_init, w_in, w_out, conv_w, subln_w,
                  ln_g, ln_b, f1g, f1u, f1d, f2g, f2u, f2d):
    x = layer_norm(ALPHA * x + 0.5 * swiglu(x, f1g, f1u, f1d), ln_g[0], ln_b[0])
    b, t, _ = x.shape
    h = x @ w_in
    bg, cg, xin, q, k, v = jnp.split(h, SPLITS, axis=-1)
    y_conv, conv_state = short_conv(cg * xin, conv_prev, conv_w)
    z_conv = bg * y_conv
    q = q.reshape(b, t, N_HEADS, 2, QK_DIM)
    k = k.reshape(b, t, N_HEADS, 2, QK_DIM)
    v = v.reshape(b, t, N_HEADS, V_DIM)
    z_attn = head_out(attn_core(q, k, v), subln_w, lam_init, x.dtype)
    mix = jnp.concatenate([z_conv, z_attn], axis=-1) @ w_out
    x = layer_norm(ALPHA * x + mix, ln_g[1], ln_b[1])
    x = layer_norm(ALPHA * x + 0.5 * swiglu(x, f2g, f2u, f2d), ln_g[2], ln_b[2])
    return x, k.reshape(b, t, N_HEADS, 2 * QK_DIM), v, conv_state


def setup_inputs(seed: int = 0) -> dict:
    key = jax.random.key(seed)
    ks = jax.random.split(key, 24)
    f32 = jnp.float32
    n_pages = PAST_LEN // PAGE_SIZE
    n_phys = (DEC_BATCH * n_pages * 5) // 4
    nrm = lambda k, shape, s: jax.random.normal(k, shape, f32) * s
    page_table = jax.random.permutation(ks[5], n_phys)[:DEC_BATCH * n_pages]
    page_table = page_table.reshape(DEC_BATCH, n_pages).astype(jnp.int32)
    return {
        "x_prompt": nrm(ks[0], (BATCH, SEQ, D_MODEL), 1.0),
        "x_sample": nrm(ks[1], (DEC_BATCH, DEC_SEQ, D_MODEL), 1.0),
        "cache_k": nrm(ks[2], (DEPTH, n_phys, PAGE_SIZE, N_HEADS, 2 * QK_DIM), 1.0),
        "cache_v": nrm(ks[3], (DEPTH, n_phys, PAGE_SIZE, N_HEADS, V_DIM), 1.0),
        "state_conv": nrm(ks[4], (DEPTH, DEC_BATCH, CONV_K - 1, CONV_CH), 1.0),
        "page_table": page_table,
        "rel_bias_table": nrm(ks[6], (NUM_BUCKETS, N_HEADS), 0.5),
        "w_in": nrm(ks[7], (DEPTH, D_MODEL, D_IN), D_MODEL ** -0.5),
        "w_out": nrm(ks[8], (DEPTH, MIX_W, D_MODEL), BETA * MIX_W ** -0.5),
        "conv_w": nrm(ks[9], (DEPTH, CONV_K, CONV_CH), CONV_K ** -0.5),
        "lambda_q1": nrm(ks[10], (DEPTH, QK_DIM), 0.1),
        "lambda_k1": nrm(ks[11], (DEPTH, QK_DIM), 0.1),
        "lambda_q2": nrm(ks[12], (DEPTH, QK_DIM), 0.1),
        "lambda_k2": nrm(ks[13], (DEPTH, QK_DIM), 0.1),
        "subln_w": 1.0 + nrm(ks[14], (DEPTH, V_DIM), 0.01),
        "ln_g": 1.0 + nrm(ks[15], (DEPTH, 3, D_MODEL), 0.01),
        "ln_b": nrm(ks[16], (DEPTH, 3, D_MODEL), 0.01),
        "ffn1_w_gate": nrm(ks[17], (DEPTH, D_MODEL, D_FF), D_MODEL ** -0.5),
        "ffn1_w_up": nrm(ks[18], (DEPTH, D_MODEL, D_FF), D_MODEL ** -0.5),
        "ffn1_w_down": nrm(ks[19], (DEPTH, D_FF, D_MODEL), BETA * D_FF ** -0.5),
        "ffn2_w_gate": nrm(ks[20], (DEPTH, D_MODEL, D_FF), D_MODEL ** -0.5),
        "ffn2_w_up": nrm(ks[21], (DEPTH, D_MODEL, D_FF), D_MODEL ** -0.5),
        "ffn2_w_down": nrm(ks[22], (DEPTH, D_FF, D_MODEL), BETA * D_FF ** -0.5),
    }


def reference(x_prompt, x_sample, cache_k, cache_v, state_conv, page_table, rel_bias_table,
              w_in, w_out, conv_w, lambda_q1, lambda_k1, lambda_q2, lambda_k2, subln_w,
              ln_g, ln_b, ffn1_w_gate, ffn1_w_up, ffn1_w_down,
              ffn2_w_gate, ffn2_w_up, ffn2_w_down):
    xp, xs = x_prompt, x_sample
    n_dec = x_sample.shape[0]
    kp_rows, vp_rows, cp_states = [], [], []
    ks_rows, vs_rows, cs_states = [], [], []
    for l in range(DEPTH):
        lam_init = 0.8 - 0.6 * math.exp(-0.3 * l)
        lam = lambda_full(lambda_q1[l], lambda_k1[l], lambda_q2[l], lambda_k2[l], lam_init)
        wts = (w_in[l], w_out[l], conv_w[l], subln_w[l], ln_g[l], ln_b[l],
               ffn1_w_gate[l], ffn1_w_up[l], ffn1_w_down[l],
               ffn2_w_gate[l], ffn2_w_up[l], ffn2_w_down[l])

        def prompt_core(q, k, v, lam=lam):
            return prompt_attention(q, k, v, lam, rel_bias_table)
        conv0 = jnp.zeros((xp.shape[0], CONV_K - 1, CONV_CH), xp.dtype)
        xp, kp, vp, cp = layer_forward(xp, conv0, prompt_core, lam_init, *wts)
        kp_rows.append(kp); vp_rows.append(vp); cp_states.append(cp)

        k_past = cache_k[l][page_table].reshape(n_dec, -1, N_HEADS, 2, QK_DIM)
        v_past = cache_v[l][page_table].reshape(n_dec, -1, N_HEADS, V_DIM)

        def sample_core(q, k, v, lam=lam, k_past=k_past, v_past=v_past):
            return sample_attention(q, k, v, k_past, v_past, lam, rel_bias_table)
        xs, ksn, vsn, csn = layer_forward(xs, state_conv[l], sample_core, lam_init, *wts)
        ks_rows.append(ksn); vs_rows.append(vsn); cs_states.append(csn)

    k_prompt = jnp.stack(kp_rows)
    v_prompt = jnp.stack(vp_rows)
    conv_prompt = jnp.stack(cp_states)
    k_sample = jnp.stack(ks_rows)
    v_sample = jnp.stack(vs_rows)
    conv_sample = jnp.stack(cs_states)
    return (xp, xs, k_prompt, v_prompt, conv_prompt, k_sample, v_sample, conv_sample)
```

```python
import functools
import math

import jax
import jax.numpy as jnp
from jax import lax
from jax.experimental import pallas as pl
from jax.experimental.pallas import tpu as pltpu

F32 = jnp.float32
BF16 = jnp.bfloat16

D_MODEL = 1024
DEPTH = 2
CONV_CH = D_MODEL // 2
CONV_K = 3
N_HEADS = 4
QK_DIM = 64
V_DIM = 2 * QK_DIM
HEAD_W = 2 * QK_DIM
ATTN_W = N_HEADS * V_DIM
D_FF = 2816
NUM_BUCKETS = 32
MAX_DISTANCE = 128
LN_EPS = 1e-5
ATTN_SCALE = QK_DIM ** -0.5
ALPHA = (2 * DEPTH) ** 0.25
MASKED = -1e30

VMEM_LIMIT_BYTES = 56 * 1024 * 1024
TOKEN_TILE = 512
FF_CHUNK = 256
Q_TILE = 512
SUBLANES = 8


def _params(n_axes):
    return pltpu.CompilerParams(dimension_semantics=("arbitrary",) * n_axes,
                                vmem_limit_bytes=VMEM_LIMIT_BYTES)


def _resident(shape):
    return pl.BlockSpec(shape, lambda *_: (0,) * len(shape), pipeline_mode=pl.Buffered(1))


def _layer_norm(x, g, b):
    mu = jnp.mean(x, axis=-1, keepdims=True)
    xc = x - mu
    var = jnp.mean(xc * xc, axis=-1, keepdims=True)
    return xc * lax.rsqrt(var + LN_EPS) * g + b


def _ffn_ln_kernel(x_ref, wg_ref, wu_ref, wd_ref, g_ref, b_ref, o_ref, h_scr):
    x = x_ref[...]
    xb = x.astype(BF16)
    for c in range(D_FF // FF_CHUNK):
        sl = slice(c * FF_CHUNK, (c + 1) * FF_CHUNK)
        g = jnp.dot(xb, wg_ref[:, sl], preferred_element_type=F32)
        u = jnp.dot(xb, wu_ref[:, sl], preferred_element_type=F32)
        h_scr[:, sl] = (g * jax.nn.sigmoid(g) * u).astype(BF16)
    y = jnp.dot(h_scr[...], wd_ref[...], preferred_element_type=F32)
    o_ref[...] = _layer_norm(ALPHA * x + 0.5 * y, g_ref[...], b_ref[...])


def _ffn_ln(x, wg, wu, wd, g, b):
    t = x.shape[0]
    tm = min(TOKEN_TILE, t)
    row = pl.BlockSpec((tm, D_MODEL), lambda i: (i, 0))
    return pl.pallas_call(
        _ffn_ln_kernel,
        grid=(t // tm,),
        in_specs=[row, _resident((D_MODEL, D_FF)), _resident((D_MODEL, D_FF)),
                  _resident((D_FF, D_MODEL)), _resident((1, D_MODEL)), _resident((1, D_MODEL))],
        out_specs=row,
        out_shape=jax.ShapeDtypeStruct((t, D_MODEL), F32),
        scratch_shapes=[pltpu.VMEM((tm, D_FF), BF16)],
        compiler_params=_params(1),
        name="ffn_ln",
    )(x, wg, wu, wd, g, b)


def _section(xb, w_ref, c):
    return jnp.dot(xb, w_ref[:, c * CONV_CH:(c + 1) * CONV_CH], preferred_element_type=F32)


def _emit_qkv(xb, w_ref, q_ref, k_ref, v_ref, kb_ref, vb_ref):
    q_ref[...] = (_section(xb, w_ref, 3) * ATTN_SCALE).astype(q_ref.dtype)
    k = _section(xb, w_ref, 4)
    k_ref[...] = k
    v = _section(xb, w_ref, 5)
    v_ref[...] = v
    if kb_ref is not None:
        kb_ref[...] = k.astype(BF16)
        vb_ref[...] = v.astype(BF16)


def _proj_in_prompt_kernel(x_ref, w_ref, cw_ref, zc_ref, q_ref, k_ref, v_ref, kb_ref, vb_ref,
                           cs_ref, u_scr, *, tiles_per_seq):
    tm = x_ref.shape[0]
    xb = x_ref[...].astype(BF16)
    bg = _section(xb, w_ref, 0)
    u = _section(xb, w_ref, 1) * _section(xb, w_ref, 2)

    @pl.when(pl.program_id(0) % tiles_per_seq == 0)
    def _():
        u_scr[0:SUBLANES, :] = jnp.zeros((SUBLANES, CONV_CH), F32)

    u_scr[SUBLANES:SUBLANES + tm, :] = u
    um1 = u_scr[SUBLANES - 1:SUBLANES - 1 + tm, :]
    um2 = u_scr[SUBLANES - 2:SUBLANES - 2 + tm, :]
    cw = cw_ref[...]
    y = cw[0:1] * um2 + cw[1:2] * um1 + cw[2:3] * u
    zc_ref[...] = (bg * y).astype(zc_ref.dtype)
    u_scr[0:SUBLANES, :] = u_scr[tm:tm + SUBLANES, :]
    cs_ref[0] = u[tm - (CONV_K - 1):tm, :]
    _emit_qkv(xb, w_ref, q_ref, k_ref, v_ref, kb_ref, vb_ref)


def _proj_in_prompt(x, w_in, conv_w, batch, seq):
    t = x.shape[0]
    tm = TOKEN_TILE
    tiles_per_seq = seq // tm
    row = lambda w: pl.BlockSpec((tm, w), lambda i: (i, 0))
    return pl.pallas_call(
        functools.partial(_proj_in_prompt_kernel, tiles_per_seq=tiles_per_seq),
        grid=(t // tm,),
        in_specs=[row(D_MODEL), _resident(w_in.shape), _resident(conv_w.shape)],
        out_specs=[row(CONV_CH), row(ATTN_W), row(ATTN_W), row(ATTN_W), row(ATTN_W), row(ATTN_W),
                   pl.BlockSpec((1, CONV_K - 1, CONV_CH), lambda i: (i // tiles_per_seq, 0, 0))],
        out_shape=[jax.ShapeDtypeStruct((t, CONV_CH), BF16),
                   jax.ShapeDtypeStruct((t, ATTN_W), BF16),
                   jax.ShapeDtypeStruct((t, ATTN_W), F32),
                   jax.ShapeDtypeStruct((t, ATTN_W), F32),
                   jax.ShapeDtypeStruct((t, ATTN_W), BF16),
                   jax.ShapeDtypeStruct((t, ATTN_W), BF16),
                   jax.ShapeDtypeStruct((batch, CONV_K - 1, CONV_CH), F32)],
        scratch_shapes=[pltpu.VMEM((tm + 2 * SUBLANES, CONV_CH), F32)],
        compiler_params=_params(1),
        name="proj_in_prompt",
    )(x, w_in, conv_w)


def _proj_in_sample_kernel(x_ref, w_ref, cw_ref, p1_ref, p2_ref, zc_ref, q_ref, k_ref, v_ref,
                           cs_ref, u_scr, *, dec_seq):
    tm = x_ref.shape[0]
    xb = x_ref[...].astype(BF16)
    bg = _section(xb, w_ref, 0)
    u = _section(xb, w_ref, 1) * _section(xb, w_ref, 2)
    u_scr[0:SUBLANES, :] = jnp.zeros((SUBLANES, CONV_CH), F32)
    u_scr[SUBLANES:SUBLANES + tm, :] = u
    tpos = lax.broadcasted_iota(jnp.int32, (tm, CONV_CH), 0) % dec_seq
    um1 = jnp.where(tpos >= 1, u_scr[SUBLANES - 1:SUBLANES - 1 + tm, :], p1_ref[...])
    um2 = jnp.where(tpos >= 2, u_scr[SUBLANES - 2:SUBLANES - 2 + tm, :], p2_ref[...])
    cw = cw_ref[...]
    y = cw[0:1] * um2 + cw[1:2] * um1 + cw[2:3] * u
    zc_ref[...] = (bg * y).astype(zc_ref.dtype)
    cs_ref[...] = u.reshape(tm // dec_seq, dec_seq, CONV_CH)[:, dec_seq - (CONV_K - 1):, :]
    _emit_qkv(xb, w_ref, q_ref, k_ref, v_ref, None, None)


def _proj_in_sample(x, w_in, conv_w, prev, dec_seq):
    t = x.shape[0]
    n_seq = t // dec_seq
    tm = min(TOKEN_TILE, t)
    p2 = jnp.pad(prev, ((0, 0), (0, dec_seq - (CONV_K - 1)), (0, 0))).reshape(t, CONV_CH)
    p1 = jnp.pad(prev[:, 1:], ((0, 0), (0, dec_seq - 1), (0, 0))).reshape(t, CONV_CH)
    row = lambda w: pl.BlockSpec((tm, w), lambda i: (i, 0))
    return pl.pallas_call(
        functools.partial(_proj_in_sample_kernel, dec_seq=dec_seq),
        grid=(t // tm,),
        in_specs=[row(D_MODEL), _resident(w_in.shape), _resident(conv_w.shape),
                  row(CONV_CH), row(CONV_CH)],
        out_specs=[row(CONV_CH), row(ATTN_W), row(ATTN_W), row(ATTN_W),
                   pl.BlockSpec((tm // dec_seq, CONV_K - 1, CONV_CH), lambda i: (i, 0, 0))],
        out_shape=[jax.ShapeDtypeStruct((t, CONV_CH), BF16),
                   jax.ShapeDtypeStruct((t, ATTN_W), F32),
                   jax.ShapeDtypeStruct((t, ATTN_W), F32),
                   jax.ShapeDtypeStruct((t, ATTN_W), F32),
                   jax.ShapeDtypeStruct((n_seq, CONV_K - 1, CONV_CH), F32)],
        scratch_shapes=[pltpu.VMEM((tm + 2 * SUBLANES, CONV_CH), F32)],
        compiler_params=_params(1),
        name="proj_in_sample",
    )(x, w_in, conv_w, p1, p2)


def _bias_kernel(tab_ref, o_ref, *, rmod, offset, cmax):
    h = pl.program_id(0)
    shape = o_ref.shape[1:]
    r = lax.broadcasted_iota(jnp.int32, shape, 0)
    c = lax.broadcasted_iota(jnp.int32, shape, 1)
    dist = (r % rmod) - c + offset
    n = jnp.maximum(dist, 0)
    max_exact = NUM_BUCKETS // 2
    nf = jnp.maximum(n, 1).astype(F32)
    large = max_exact + (jnp.log(nf / max_exact) / math.log(MAX_DISTANCE / max_exact)
                         * (NUM_BUCKETS - max_exact)).astype(jnp.int32)
    large = jnp.minimum(large, NUM_BUCKETS - 1)
    bucket = jnp.where(n < max_exact, n, large)
    last = tab_ref[NUM_BUCKETS - 1, h]
    bias = jnp.zeros(shape, F32)
    for b in range(NUM_BUCKETS - 1):
        bias = jnp.where(bucket == b, tab_ref[b, h] - last, bias)
    o_ref[0] = jnp.where((dist >= 0) & (c < cmax), bias, MASKED)


def _bias_tiles(table, rows, cols, rmod, offset, cmax):
    return pl.pallas_call(
        functools.partial(_bias_kernel, rmod=rmod, offset=offset, cmax=cmax),
        grid=(N_HEADS,),
        in_specs=[pl.BlockSpec(memory_space=pltpu.SMEM)],
        out_specs=pl.BlockSpec((1, rows, cols), lambda h: (h, 0, 0)),
        out_shape=jax.ShapeDtypeStruct((N_HEADS, rows, cols), F32),
        compiler_params=_params(1),
        name="rel_bias_tiles",
    )(table)


def _softmax_step(s, v, m_scr, l_scr, acc_scr):
    m_prev = m_scr[...]
    m_new = jnp.maximum(m_prev, jnp.max(s, axis=-1, keepdims=True))
    alpha = jnp.exp(m_prev - m_new)
    p = jnp.exp(s - m_new)
    l_scr[...] = alpha * l_scr[...] + jnp.sum(p, axis=-1, keepdims=True)
    acc_scr[...] = alpha * acc_scr[...] + jnp.dot(p.astype(BF16), v, preferred_element_type=F32)
    m_scr[...] = m_new


def _init_softmax(m_scr, l_scr, acc_scr):
    m_scr[...] = jnp.full(m_scr.shape, MASKED, F32)
    l_scr[...] = jnp.zeros(l_scr.shape, F32)
    acc_scr[...] = jnp.zeros(acc_scr.shape, F32)


def _lambda_full(lam_ref, lam_init):
    lp = lam_ref[...]
    d1 = jnp.sum(lp[0:1] * lp[1:2], axis=-1, keepdims=True)
    d2 = jnp.sum(lp[2:3] * lp[3:4], axis=-1, keepdims=True)
    return jnp.exp(d1) - jnp.exp(d2) + lam_init


def _head_out(o1, o2, lam, subln, lam_init):
    od = o1 - lam * o2
    ms = jnp.mean(od * od, axis=-1, keepdims=True)
    return od * lax.rsqrt(ms + LN_EPS) * subln * (1.0 - lam_init)


def _attn_prompt_kernel(lam_ref, subln_ref, q_ref, k_ref, v_ref, band_ref, o_ref,
                        qs_scr, m_scr, l_scr, acc_scr, *, lam_init):
    tq = q_ref.shape[0]
    i = pl.program_id(2)
    q = q_ref[...]
    lane = lax.broadcasted_iota(jnp.int32, q.shape, 1)
    zero = jnp.zeros_like(q)
    qs_scr[0:tq, :] = jnp.where(lane < QK_DIM, q, zero)
    qs_scr[tq:2 * tq, :] = jnp.where(lane >= QK_DIM, q, zero)
    _init_softmax(m_scr, l_scr, acc_scr)

    def tile(start, width, bias):
        k = k_ref[pl.ds(start, width), :]
        v = v_ref[pl.ds(start, width), :]
        s = lax.dot_general(qs_scr[...], k, (((1,), (1,)), ((), ())), preferred_element_type=F32)
        if bias is not None:
            s = (s.reshape(2, tq, width) + bias[None]).reshape(2 * tq, width)
        _softmax_step(s, v, m_scr, l_scr, acc_scr)

    n_far = jnp.maximum(i - 1, 0)

    def far_body(j, carry):
        tile(pl.multiple_of(j * tq, tq), tq, None)
        return carry

    lax.fori_loop(0, n_far, far_body, 0)
    tile(pl.multiple_of(n_far * tq, tq), 2 * tq, band_ref[0, jnp.minimum(i, 1)])

    o = acc_scr[...] / l_scr[...]
    z = _head_out(o[0:tq], o[tq:2 * tq], _lambda_full(lam_ref, lam_init), subln_ref[...], lam_init)
    o_ref[...] = z.astype(o_ref.dtype)


def _attn_prompt(q, kb, vb, band, lam_p, subln, lam_init, batch, seq):
    t = q.shape[0]
    tq = Q_TILE
    nq = seq // tq
    kv_spec = pl.BlockSpec((seq, HEAD_W), lambda b, h, i: (b, h))
    q_spec = pl.BlockSpec((tq, HEAD_W), lambda b, h, i: (b * nq + i, h))
    return pl.pallas_call(
        functools.partial(_attn_prompt_kernel, lam_init=lam_init),
        grid=(batch, N_HEADS, nq),
        in_specs=[pl.BlockSpec((4, QK_DIM), lambda b, h, i: (0, 0)),
                  pl.BlockSpec((1, V_DIM), lambda b, h, i: (0, 0)),
                  q_spec, kv_spec, kv_spec,
                  pl.BlockSpec((1, 2, tq, 2 * tq), lambda b, h, i: (h, 0, 0, 0))],
        out_specs=q_spec,
        out_shape=jax.ShapeDtypeStruct((t, ATTN_W), BF16),
        scratch_shapes=[pltpu.VMEM((2 * tq, HEAD_W), BF16),
                        pltpu.VMEM((2 * tq, 1), F32), pltpu.VMEM((2 * tq, 1), F32),
                        pltpu.VMEM((2 * tq, V_DIM), F32)],
        compiler_params=_params(3),
        name="attn_prompt",
    )(lam_p, subln, q, kb, vb, band)


def _attn_sample_kernel(pt_ref, lam_ref, subln_ref, q_ref, kn_ref, vn_ref, kp_ref, vp_ref,
                        lastb_ref, newb_ref, o_ref, qbd_scr, kn_scr, vn_scr, m_scr, l_scr, acc_scr,
                        *, lam_init, n_pages):
    del pt_ref
    p = pl.program_id(1)
    dec_seq = q_ref.shape[1]
    rows = qbd_scr.shape[0]

    @pl.when(p == 0)
    def _():
        qt = jnp.concatenate([q_ref[0]] * (rows // dec_seq), axis=0)
        rblk = lax.broadcasted_iota(jnp.int32, qt.shape, 0) // dec_seq
        cblk = lax.broadcasted_iota(jnp.int32, qt.shape, 1) // QK_DIM
        qbd_scr[...] = jnp.where(rblk == cblk, qt, 0.0).astype(BF16)
        _init_softmax(m_scr, l_scr, acc_scr)

    def logits(k):
        return lax.dot_general(qbd_scr[...], k, (((1,), (1,)), ((), ())), preferred_element_type=F32)

    s = logits(kp_ref[0, 0].astype(BF16))
    s = s + jnp.where(p == n_pages - 1, lastb_ref[...], 0.0)
    _softmax_step(s, vp_ref[0, 0].astype(BF16), m_scr, l_scr, acc_scr)

    @pl.when(p == n_pages - 1)
    def _():
        kn_scr[...] = jnp.zeros(kn_scr.shape, F32)
        vn_scr[...] = jnp.zeros(vn_scr.shape, F32)
        kn_scr[0:dec_seq, :] = kn_ref[0]
        vn_scr[0:dec_seq, :] = vn_ref[0]
        s_new = logits(kn_scr[...].astype(BF16)) + newb_ref[...]
        _softmax_step(s_new, vn_scr[...].astype(BF16), m_scr, l_scr, acc_scr)
        o = acc_scr[...] / l_scr[...]
        lam = _lambda_full(lam_ref, lam_init)
        heads = []
        for h in range(N_HEADS):
            oh = o[2 * dec_seq * h:2 * dec_seq * (h + 1), V_DIM * h:V_DIM * (h + 1)]
            heads.append(_head_out(oh[0:dec_seq], oh[dec_seq:2 * dec_seq], lam, subln_ref[...],
                                   lam_init))
        o_ref[0] = jnp.concatenate(heads, axis=-1)


def _attn_sample(q, k_new, v_new, cache_k, cache_v, layer, page_table, lastb, newb, lam_p, subln,
                 lam_init):
    n_seq, dec_seq, _ = q.shape
    n_pages = page_table.shape[1]
    page = cache_k.shape[2]
    rows = N_HEADS * 2 * dec_seq
    seq_spec = pl.BlockSpec((1, dec_seq, ATTN_W), lambda s, p, pt: (s, 0, 0))
    page_spec = pl.BlockSpec((1, 1, page, ATTN_W),
                             lambda s, p, pt: (layer, pt[s * n_pages + p], 0, 0))
    const = lambda shape: pl.BlockSpec(shape, lambda s, p, pt: (0,) * len(shape))
    grid_spec = pltpu.PrefetchScalarGridSpec(
        num_scalar_prefetch=1,
        grid=(n_seq, n_pages),
        in_specs=[const((4, QK_DIM)), const((1, V_DIM)), seq_spec, seq_spec, seq_spec,
                  page_spec, page_spec, const((rows, page)), const((rows, page))],
        out_specs=seq_spec,
        scratch_shapes=[pltpu.VMEM((rows, ATTN_W), BF16),
                        pltpu.VMEM((page, ATTN_W), F32), pltpu.VMEM((page, ATTN_W), F32),
                        pltpu.VMEM((rows, 1), F32), pltpu.VMEM((rows, 1), F32),
                        pltpu.VMEM((rows, ATTN_W), F32)],
    )
    return pl.pallas_call(
        functools.partial(_attn_sample_kernel, lam_init=lam_init, n_pages=n_pages),
        grid_spec=grid_spec,
        out_shape=jax.ShapeDtypeStruct((n_seq, dec_seq, ATTN_W), F32),
        compiler_params=_params(2),
        name="attn_sample",
    )(page_table.reshape(-1), lam_p, subln, q, k_new, v_new, cache_k, cache_v, lastb, newb)


def _proj_out_ln_kernel(x_ref, zc_ref, za_ref, wc_ref, wa_ref, g_ref, b_ref, o_ref):
    mix = (jnp.dot(zc_ref[...].astype(BF16), wc_ref[...], preferred_element_type=F32)
           + jnp.dot(za_ref[...].astype(BF16), wa_ref[...], preferred_element_type=F32))
    o_ref[...] = _layer_norm(ALPHA * x_ref[...] + mix, g_ref[...], b_ref[...])


def _proj_out_ln(x, zc, za, w_conv, w_attn, g, b):
    t = x.shape[0]
    tm = min(TOKEN_TILE, t)
    row = lambda w: pl.BlockSpec((tm, w), lambda i: (i, 0))
    return pl.pallas_call(
        _proj_out_ln_kernel,
        grid=(t // tm,),
        in_specs=[row(D_MODEL), row(CONV_CH), row(ATTN_W), _resident(w_conv.shape),
                  _resident(w_attn.shape), _resident((1, D_MODEL)), _resident((1, D_MODEL))],
        out_specs=row(D_MODEL),
        out_shape=jax.ShapeDtypeStruct((t, D_MODEL), F32),
        compiler_params=_params(1),
        name="proj_out_ln",
    )(x, zc, za, w_conv, w_attn, g, b)


def kernel(x_prompt, x_sample, cache_k, cache_v, state_conv, page_table, rel_bias_table,
           w_in, w_out, conv_w, lambda_q1, lambda_k1, lambda_q2, lambda_k2, subln_w,
           ln_g, ln_b, ffn1_w_gate, ffn1_w_up, ffn1_w_down,
           ffn2_w_gate, ffn2_w_up, ffn2_w_down):
    batch, seq, _ = x_prompt.shape
    n_dec, dec_seq, _ = x_sample.shape
    depth, n_phys, page = cache_k.shape[:3]
    past = page_table.shape[1] * page

    xp = x_prompt.reshape(batch * seq, D_MODEL)
    xs = x_sample.reshape(n_dec * dec_seq, D_MODEL)
    ck = cache_k.reshape(depth, n_phys, page, ATTN_W)
    cv = cache_v.reshape(depth, n_phys, page, ATTN_W)
    bf = lambda w: w.astype(BF16)

    band = jnp.stack([_bias_tiles(rel_bias_table, Q_TILE, 2 * Q_TILE, Q_TILE, off, 2 * Q_TILE)
                      for off in (0, Q_TILE)], axis=1)
    rows = N_HEADS * 2 * dec_seq
    lastb = _bias_tiles(rel_bias_table, 2 * dec_seq, page, dec_seq, page, page).reshape(rows, page)
    newb = _bias_tiles(rel_bias_table, 2 * dec_seq, page, dec_seq, 0, dec_seq).reshape(rows, page)
    del past

    outs = {name: [] for name in ("kp", "vp", "cp", "ks", "vs", "cs")}
    for l in range(depth):
        lam_init = 0.8 - 0.6 * math.exp(-0.3 * l)
        lam_p = jnp.stack([lambda_q1[l], lambda_k1[l], lambda_q2[l], lambda_k2[l]])
        subln = subln_w[l].reshape(1, V_DIM)
        g = lambda j: ln_g[l, j].reshape(1, D_MODEL)
        b = lambda j: ln_b[l, j].reshape(1, D_MODEL)
        f1 = (bf(ffn1_w_gate[l]), bf(ffn1_w_up[l]), bf(ffn1_w_down[l]))
        f2 = (bf(ffn2_w_gate[l]), bf(ffn2_w_up[l]), bf(ffn2_w_down[l]))
        wi = bf(w_in[l])
        wo_conv, wo_attn = bf(w_out[l, :CONV_CH]), bf(w_out[l, CONV_CH:])

        xp = _ffn_ln(xp, *f1, g(0), b(0))
        zc, q, k, v, kb, vb, cst = _proj_in_prompt(xp, wi, conv_w[l], batch, seq)
        za = _attn_prompt(q, kb, vb, band, lam_p, subln, lam_init, batch, seq)
        xp = _proj_out_ln(xp, zc, za, wo_conv, wo_attn, g(1), b(1))
        xp = _ffn_ln(xp, *f2, g(2), b(2))
        outs["kp"].append(k.reshape(batch, seq, N_HEADS, HEAD_W))
        outs["vp"].append(v.reshape(batch, seq, N_HEADS, V_DIM))
        outs["cp"].append(cst)

        xs = _ffn_ln(xs, *f1, g(0), b(0))
        zc, q, k, v, cst = _proj_in_sample(xs, wi, conv_w[l], state_conv[l], dec_seq)
        seq3 = lambda a: a.reshape(n_dec, dec_seq, ATTN_W)
        za = _attn_sample(seq3(q), seq3(k), seq3(v), ck, cv, l, page_table, lastb, newb, lam_p,
                          subln, lam_init)
        xs = _proj_out_ln(xs, zc, za.reshape(n_dec * dec_seq, ATTN_W), wo_conv, wo_attn, g(1), b(1))
        xs = _ffn_ln(xs, *f2, g(2), b(2))
        outs["ks"].append(k.reshape(n_dec, dec_seq, N_HEADS, HEAD_W))
        outs["vs"].append(v.reshape(n_dec, dec_seq, N_HEADS, V_DIM))
        outs["cs"].append(cst)

    return (xp.reshape(batch, seq, D_MODEL), xs.reshape(n_dec, dec_seq, D_MODEL),
            jnp.stack(outs["kp"]), jnp.stack(outs["vp"]), jnp.stack(outs["cp"]),
            jnp.stack(outs["ks"]), jnp.stack(outs["vs"]), jnp.stack(outs["cs"]))
```

```python
import functools
import math

import jax
import jax.numpy as jnp
from jax import lax
from jax.experimental import pallas as pl
from jax.experimental.pallas import tpu as pltpu

F32 = jnp.float32
BF16 = jnp.bfloat16

D_MODEL = 1024
DEPTH = 2
CONV_CH = D_MODEL // 2
CONV_K = 3
N_HEADS = 4
QK_DIM = 64
V_DIM = 2 * QK_DIM
HEAD_W = 2 * QK_DIM
ATTN_W = N_HEADS * V_DIM
D_FF = 2816
NUM_BUCKETS = 32
MAX_DISTANCE = 128
LN_EPS = 1e-5
ATTN_SCALE = QK_DIM ** -0.5
ALPHA = (2 * DEPTH) ** 0.25
MASKED = -1e30

VMEM_LIMIT_BYTES = 56 * 1024 * 1024
TOKEN_TILE = 512
FF_CHUNK = 256
Q_TILE = 512
SUBLANES = 8


def _params(n_axes):
    return pltpu.CompilerParams(dimension_semantics=("arbitrary",) * n_axes,
                                vmem_limit_bytes=VMEM_LIMIT_BYTES)


def _resident(shape):
    return pl.BlockSpec(shape, lambda *_: (0,) * len(shape), pipeline_mode=pl.Buffered(1))


def _layer_norm(x, g, b):
    mu = jnp.mean(x, axis=-1, keepdims=True)
    xc = x - mu
    var = jnp.mean(xc * xc, axis=-1, keepdims=True)
    return xc * lax.rsqrt(var + LN_EPS) * g + b


def _ffn_ln_kernel(x_ref, wg_ref, wu_ref, wd_ref, g_ref, b_ref, o_ref, h_scr):
    x = x_ref[...]
    xb = x.astype(BF16)
    for c in range(D_FF // FF_CHUNK):
        sl = slice(c * FF_CHUNK, (c + 1) * FF_CHUNK)
        g = jnp.dot(xb, wg_ref[:, sl], preferred_element_type=F32)
        u = jnp.dot(xb, wu_ref[:, sl], preferred_element_type=F32)
        h_scr[:, sl] = (g * jax.nn.sigmoid(g) * u).astype(BF16)
    y = jnp.dot(h_scr[...], wd_ref[...], preferred_element_type=F32)
    o_ref[...] = _layer_norm(ALPHA * x + 0.5 * y, g_ref[...], b_ref[...])


def _ffn_ln(x, wg, wu, wd, g, b):
    t = x.shape[0]
    tm = min(TOKEN_TILE, t)
    row = pl.BlockSpec((tm, D_MODEL), lambda i: (i, 0))
    return pl.pallas_call(
        _ffn_ln_kernel,
        grid=(t // tm,),
        in_specs=[row, _resident((D_MODEL, D_FF)), _resident((D_MODEL, D_FF)),
                  _resident((D_FF, D_MODEL)), _resident((1, D_MODEL)), _resident((1, D_MODEL))],
        out_specs=row,
        out_shape=jax.ShapeDtypeStruct((t, D_MODEL), F32),
        scratch_shapes=[pltpu.VMEM((tm, D_FF), BF16)],
        compiler_params=_params(1),
        name="ffn_ln",
    )(x, wg, wu, wd, g, b)


def _section(xb, w_ref, c):
    return jnp.dot(xb, w_ref[:, c * CONV_CH:(c + 1) * CONV_CH], preferred_element_type=F32)


def _emit_qkv(xb, w_ref, q_ref, k_ref, v_ref, kb_ref, vb_ref):
    q_ref[...] = (_section(xb, w_ref, 3) * ATTN_SCALE).astype(q_ref.dtype)
    k = _section(xb, w_ref, 4)
    k_ref[...] = k
    v = _section(xb, w_ref, 5)
    v_ref[...] = v
    if kb_ref is not None:
        kb_ref[...] = k.astype(BF16)
        vb_ref[...] = v.astype(BF16)


def _proj_in_prompt_kernel(x_ref, w_ref, cw_ref, zc_ref, q_ref, k_ref, v_ref, kb_ref, vb_ref,
                           cs_ref, u_scr, *, tiles_per_seq):
    tm = x_ref.shape[0]
    xb = x_ref[...].astype(BF16)
    bg = _section(xb, w_ref, 0)
    u = _section(xb, w_ref, 1) * _section(xb, w_ref, 2)

    @pl.when(pl.program_id(0) % tiles_per_seq == 0)
    def _():
        u_scr[0:SUBLANES, :] = jnp.zeros((SUBLANES, CONV_CH), F32)

    u_scr[SUBLANES:SUBLANES + tm, :] = u
    um1 = u_scr[SUBLANES - 1:SUBLANES - 1 + tm, :]
    um2 = u_scr[SUBLANES - 2:SUBLANES - 2 + tm, :]
    cw = cw_ref[...]
    y = cw[0:1] * um2 + cw[1:2] * um1 + cw[2:3] * u
    zc_ref[...] = (bg * y).astype(zc_ref.dtype)
    u_scr[0:SUBLANES, :] = u_scr[tm:tm + SUBLANES, :]
    cs_ref[0] = u[tm - (CONV_K - 1):tm, :]
    _emit_qkv(xb, w_ref, q_ref, k_ref, v_ref, kb_ref, vb_ref)


def _proj_in_prompt(x, w_in, conv_w, batch, seq):
    t = x.shape[0]
    tm = TOKEN_TILE
    tiles_per_seq = seq // tm
    row = lambda w: pl.BlockSpec((tm, w), lambda i: (i, 0))
    return pl.pallas_call(
        functools.partial(_proj_in_prompt_kernel, tiles_per_seq=tiles_per_seq),
        grid=(t // tm,),
        in_specs=[row(D_MODEL), _resident(w_in.shape), _resident(conv_w.shape)],
        out_specs=[row(CONV_CH), row(ATTN_W), row(ATTN_W), row(ATTN_W), row(ATTN_W), row(ATTN_W),
                   pl.BlockSpec((1, CONV_K - 1, CONV_CH), lambda i: (i // tiles_per_seq, 0, 0))],
        out_shape=[jax.ShapeDtypeStruct((t, CONV_CH), BF16),
                   jax.ShapeDtypeStruct((t, ATTN_W), BF16),
                   jax.ShapeDtypeStruct((t, ATTN_W), F32),
                   jax.ShapeDtypeStruct((t, ATTN_W), F32),
                   jax.ShapeDtypeStruct((t, ATTN_W), BF16),
                   jax.ShapeDtypeStruct((t, ATTN_W), BF16),
                   jax.ShapeDtypeStruct((batch, CONV_K - 1, CONV_CH), F32)],
        scratch_shapes=[pltpu.VMEM((tm + 2 * SUBLANES, CONV_CH), F32)],
        compiler_params=_params(1),
        name="proj_in_prompt",
    )(x, w_in, conv_w)


def _proj_in_sample_kernel(x_ref, w_ref, cw_ref, p1_ref, p2_ref, zc_ref, q_ref, k_ref, v_ref,
                           cs_ref, u_scr, *, dec_seq):
    tm = x_ref.shape[0]
    xb = x_ref[...].astype(BF16)
    bg = _section(xb, w_ref, 0)
    u = _section(xb, w_ref, 1) * _section(xb, w_ref, 2)
    u_scr[0:SUBLANES, :] = jnp.zeros((SUBLANES, CONV_CH), F32)
    u_scr[SUBLANES:SUBLANES + tm, :] = u
    tpos = lax.broadcasted_iota(jnp.int32, (tm, CONV_CH), 0) % dec_seq
    um1 = jnp.where(tpos >= 1, u_scr[SUBLANES - 1:SUBLANES - 1 + tm, :], p1_ref[...])
    um2 = jnp.where(tpos >= 2, u_scr[SUBLANES - 2:SUBLANES - 2 + tm, :], p2_ref[...])
    cw = cw_ref[...]
    y = cw[0:1] * um2 + cw[1:2] * um1 + cw[2:3] * u
    zc_ref[...] = (bg * y).astype(zc_ref.dtype)
    cs_ref[...] = u.reshape(tm // dec_seq, dec_seq, CONV_CH)[:, dec_seq - (CONV_K - 1):, :]
    _emit_qkv(xb, w_ref, q_ref, k_ref, v_ref, None, None)


def _proj_in_sample(x, w_in, conv_w, prev, dec_seq):
    t = x.shape[0]
    n_seq = t // dec_seq
    tm = min(TOKEN_TILE, t)
    p2 = jnp.pad(prev, ((0, 0), (0, dec_seq - (CONV_K - 1)), (0, 0))).reshape(t, CONV_CH)
    p1 = jnp.pad(prev[:, 1:], ((0, 0), (0, dec_seq - 1), (0, 0))).reshape(t, CONV_CH)
    row = lambda w: pl.BlockSpec((tm, w), lambda i: (i, 0))
    return pl.pallas_call(
        functools.partial(_proj_in_sample_kernel, dec_seq=dec_seq),
        grid=(t // tm,),
        in_specs=[row(D_MODEL), _resident(w_in.shape), _resident(conv_w.shape),
                  row(CONV_CH), row(CONV_CH)],
        out_specs=[row(CONV_CH), row(ATTN_W), row(ATTN_W), row(ATTN_W),
                   pl.BlockSpec((tm // dec_seq, CONV_K - 1, CONV_CH), lambda i: (i, 0, 0))],
        out_shape=[jax.ShapeDtypeStruct((t, CONV_CH), BF16),
                   jax.ShapeDtypeStruct((t, ATTN_W), F32),
                   jax.ShapeDtypeStruct((t, ATTN_W), F32),
                   jax.ShapeDtypeStruct((t, ATTN_W), F32),
                   jax.ShapeDtypeStruct((n_seq, CONV_K - 1, CONV_CH), F32)],
        scratch_shapes=[pltpu.VMEM((tm + 2 * SUBLANES, CONV_CH), F32)],
        compiler_params=_params(1),
        name="proj_in_sample",
    )(x, w_in, conv_w, p1, p2)


def _bias_kernel(tab_ref, o_ref, *, rmod, offset, cmax):
    h = pl.program_id(0)
    shape = o_ref.shape[1:]
    r = lax.broadcasted_iota(jnp.int32, shape, 0)
    c = lax.broadcasted_iota(jnp.int32, shape, 1)
    dist = (r % rmod) - c + offset
    n = jnp.maximum(dist, 0)
    max_exact = NUM_BUCKETS // 2
    nf = jnp.maximum(n, 1).astype(F32)
    large = max_exact + (jnp.log(nf / max_exact) / math.log(MAX_DISTANCE / max_exact)
                         * (NUM_BUCKETS - max_exact)).astype(jnp.int32)
    large = jnp.minimum(large, NUM_BUCKETS - 1)
    bucket = jnp.where(n < max_exact, n, large)
    last = tab_ref[NUM_BUCKETS - 1, h]
    bias = jnp.zeros(shape, F32)
    for b in range(NUM_BUCKETS - 1):
        bias = jnp.where(bucket == b, tab_ref[b, h] - last, bias)
    o_ref[0] = jnp.where((dist >= 0) & (c < cmax), bias, MASKED)


def _bias_tiles(table, rows, cols, rmod, offset, cmax):
    return pl.pallas_call(
        functools.partial(_bias_kernel, rmod=rmod, offset=offset, cmax=cmax),
        grid=(N_HEADS,),
        in_specs=[pl.BlockSpec(memory_space=pltpu.SMEM)],
        out_specs=pl.BlockSpec((1, rows, cols), lambda h: (h, 0, 0)),
        out_shape=jax.ShapeDtypeStruct((N_HEADS, rows, cols), F32),
        compiler_params=_params(1),
        name="rel_bias_tiles",
    )(table)


def _softmax_step(s, v, m_scr, l_scr, acc_scr):
    m_prev = m_scr[...]
    m_new = jnp.maximum(m_prev, jnp.max(s, axis=-1, keepdims=True))
    alpha = jnp.exp(m_prev - m_new)
    p = jnp.exp(s - m_new)
    l_scr[...] = alpha * l_scr[...] + jnp.sum(p, axis=-1, keepdims=True)
    acc_scr[...] = alpha * acc_scr[...] + jnp.dot(p.astype(BF16), v, preferred_element_type=F32)
    m_scr[...] = m_new


def _init_softmax(m_scr, l_scr, acc_scr):
    m_scr[...] = jnp.full(m_scr.shape, MASKED, F32)
    l_scr[...] = jnp.zeros(l_scr.shape, F32)
    acc_scr[...] = jnp.zeros(acc_scr.shape, F32)


def _lambda_full(lam_ref, lam_init):
    lp = lam_ref[...]
    d1 = jnp.sum(lp[0:1] * lp[1:2], axis=-1, keepdims=True)
    d2 = jnp.sum(lp[2:3] * lp[3:4], axis=-1, keepdims=True)
    return jnp.exp(d1) - jnp.exp(d2) + lam_init


def _head_out(o1, o2, lam, subln, lam_init):
    od = o1 - lam * o2
    ms = jnp.mean(od * od, axis=-1, keepdims=True)
    return od * lax.rsqrt(ms + LN_EPS) * subln * (1.0 - lam_init)


def _attn_prompt_kernel(lam_ref, subln_ref, q_ref, k_ref, v_ref, band_ref, o_ref,
                        qs_scr, m_scr, l_scr, acc_scr, *, lam_init):
    tq = q_ref.shape[0]
    i = pl.program_id(2)
    q = q_ref[...]
    lane = lax.broadcasted_iota(jnp.int32, q.shape, 1)
    zero = jnp.zeros_like(q)
    qs_scr[0:tq, :] = jnp.where(lane < QK_DIM, q, zero)
    qs_scr[tq:2 * tq, :] = jnp.where(lane >= QK_DIM, q, zero)
    _init_softmax(m_scr, l_scr, acc_scr)

    def tile(start, width, bias):
        k = k_ref[pl.ds(start, width), :]
        v = v_ref[pl.ds(start, width), :]
        s = lax.dot_general(qs_scr[...], k, (((1,), (1,)), ((), ())), preferred_element_type=F32)
        if bias is not None:
            s = (s.reshape(2, tq, width) + bias[None]).reshape(2 * tq, width)
        _softmax_step(s, v, m_scr, l_scr, acc_scr)

    n_far = jnp.maximum(i - 1, 0)

    def far_body(j, carry):
        tile(pl.multiple_of(j * tq, tq), tq, None)
        return carry

    lax.fori_loop(0, n_far, far_body, 0)
    tile(pl.multiple_of(n_far * tq, tq), 2 * tq, band_ref[0, jnp.minimum(i, 1)])

    o = acc_scr[...] / l_scr[...]
    z = _head_out(o[0:tq], o[tq:2 * tq], _lambda_full(lam_ref, lam_init), subln_ref[...], lam_init)
    o_ref[...] = z.astype(o_ref.dtype)


def _attn_prompt(q, kb, vb, band, lam_p, subln, lam_init, batch, seq):
    t = q.shape[0]
    tq = Q_TILE
    nq = seq // tq
    kv_spec = pl.BlockSpec((seq, HEAD_W), lambda b, h, i: (b, h))
    q_spec = pl.BlockSpec((tq, HEAD_W), lambda b, h, i: (b * nq + i, h))
    return pl.pallas_call(
        functools.partial(_attn_prompt_kernel, lam_init=lam_init),
        grid=(batch, N_HEADS, nq),
        in_specs=[pl.BlockSpec((4, QK_DIM), lambda b, h, i: (0, 0)),
                  pl.BlockSpec((1, V_DIM), lambda b, h, i: (0, 0)),
                  q_spec, kv_spec, kv_spec,
                  pl.BlockSpec((1, 2, tq, 2 * tq), lambda b, h, i: (h, 0, 0, 0))],
        out_specs=q_spec,
        out_shape=jax.ShapeDtypeStruct((t, ATTN_W), BF16),
        scratch_shapes=[pltpu.VMEM((2 * tq, HEAD_W), BF16),
                        pltpu.VMEM((2 * tq, 1), F32), pltpu.VMEM((2 * tq, 1), F32),
                        pltpu.VMEM((2 * tq, V_DIM), F32)],
        compiler_params=_params(3),
        name="attn_prompt",
    )(lam_p, subln, q, kb, vb, band)


def _page_copies(pt_ref, ck_ref, cv_ref, kbuf, vbuf, sem, seq, slot, layer, n_pages):
    rows = ck_ref.shape[2]
    copies = []
    for p in range(n_pages):
        phys = 0 if seq is None else pt_ref[seq * n_pages + p]
        dst = pl.ds(p * rows, rows)
        copies.append(pltpu.make_async_copy(ck_ref.at[layer, phys], kbuf.at[slot, dst], sem.at[slot, 0]))
        copies.append(pltpu.make_async_copy(cv_ref.at[layer, phys], vbuf.at[slot, dst], sem.at[slot, 1]))
    return copies


def _attn_sample_kernel(pt_ref, lam_ref, subln_ref, q_ref, kn_ref, vn_ref, pastb_ref, newb_ref,
                        ck_ref, cv_ref, o_ref, kbuf, vbuf, kn_scr, vn_scr, sem,
                        *, lam_init, layer, n_pages):
    seq = pl.program_id(0)
    slot = seq % 2
    copies = functools.partial(_page_copies, pt_ref, ck_ref, cv_ref, kbuf, vbuf, sem,
                               layer=layer, n_pages=n_pages)

    @pl.when(seq == 0)
    def _():
        for c in copies(seq=0, slot=0):
            c.start()

    @pl.when(seq + 1 < pl.num_programs(0))
    def _():
        for c in copies(seq=seq + 1, slot=1 - slot):
            c.start()

    dec_seq = q_ref.shape[1]
    q = q_ref[0]
    lane = lax.broadcasted_iota(jnp.int32, (dec_seq, HEAD_W), 1)
    blocks = []
    for h in range(N_HEADS):
        qh = q[:, h * HEAD_W:(h + 1) * HEAD_W]
        blocks += [jnp.where(lane < QK_DIM, qh, 0.0), jnp.where(lane >= QK_DIM, qh, 0.0)]
    qall = jnp.concatenate(blocks, axis=0).astype(BF16)

    def logits(k):
        return lax.dot_general(qall, k, (((1,), (1,)), ((), ())), preferred_element_type=F32)

    n_new = kn_ref.shape[1]
    kn_scr[...] = jnp.zeros(kn_scr.shape, F32)
    vn_scr[...] = jnp.zeros(vn_scr.shape, F32)
    kn_scr[0:n_new, :] = kn_ref[0]
    vn_scr[0:n_new, :] = vn_ref[0]
    s_new = logits(kn_scr[...].astype(BF16)) + newb_ref[...]

    for c in copies(seq=None, slot=slot):
        c.wait()
    s_past = logits(kbuf[slot].astype(BF16)) + pastb_ref[...]
    m = jnp.maximum(jnp.max(s_past, axis=-1, keepdims=True), jnp.max(s_new, axis=-1, keepdims=True))
    p_past = jnp.exp(s_past - m)
    p_new = jnp.exp(s_new - m)
    l = jnp.sum(p_past, axis=-1, keepdims=True) + jnp.sum(p_new, axis=-1, keepdims=True)
    acc = (jnp.dot(p_past.astype(BF16), vbuf[slot].astype(BF16), preferred_element_type=F32)
           + jnp.dot(p_new.astype(BF16), vn_scr[...].astype(BF16), preferred_element_type=F32))
    o = acc / l
    lam = _lambda_full(lam_ref, lam_init)
    heads = []
    for h in range(N_HEADS):
        oh = o[2 * dec_seq * h:2 * dec_seq * (h + 1)]
        heads.append(_head_out(oh[0:dec_seq], oh[dec_seq:2 * dec_seq], lam, subln_ref[...], lam_init))
    o_ref[0] = jnp.concatenate(heads, axis=-1)


def _attn_sample(q, k_new, v_new, cache_k, cache_v, layer, page_table, pastb, newb, lam_p, subln,
                 lam_init):
    n_seq, dec_seq, _ = q.shape
    n_pages = page_table.shape[1]
    page_rows = cache_k.shape[2]
    rows = N_HEADS * 2 * dec_seq
    past_rows = n_pages * page_rows
    per_seq = lambda a: pl.BlockSpec((1,) + a.shape[1:], lambda s, pt: (s, 0, 0))
    const = lambda shape: pl.BlockSpec(shape, lambda s, pt: (0,) * len(shape),
                                       pipeline_mode=pl.Buffered(1))
    hbm = pl.BlockSpec(memory_space=pl.ANY)
    grid_spec = pltpu.PrefetchScalarGridSpec(
        num_scalar_prefetch=1,
        grid=(n_seq,),
        in_specs=[const((4, QK_DIM)), const((1, V_DIM)), per_seq(q), per_seq(k_new), per_seq(v_new),
                  const((rows, past_rows)), const((rows, HEAD_W)), hbm, hbm],
        out_specs=per_seq(q),
        scratch_shapes=[pltpu.VMEM((2, past_rows, HEAD_W), F32), pltpu.VMEM((2, past_rows, HEAD_W), F32),
                        pltpu.VMEM((HEAD_W, HEAD_W), F32), pltpu.VMEM((HEAD_W, HEAD_W), F32),
                        pltpu.SemaphoreType.DMA((2, 2))],
    )
    return pl.pallas_call(
        functools.partial(_attn_sample_kernel, lam_init=lam_init, layer=layer, n_pages=n_pages),
        grid_spec=grid_spec,
        out_shape=jax.ShapeDtypeStruct((n_seq, dec_seq, ATTN_W), F32),
        compiler_params=_params(1),
        name="attn_sample",
    )(page_table.reshape(-1), lam_p, subln, q, k_new, v_new, pastb, newb, cache_k, cache_v)


def _sample_bias_tiles(table, dec_seq, page, n_pages):
    rows = N_HEADS * 2 * dec_seq
    lastb = _bias_tiles(table, 2 * dec_seq, page, dec_seq, page, page).reshape(rows, page)
    newb = _bias_tiles(table, 2 * dec_seq, page, dec_seq, 0, dec_seq).reshape(rows, page)
    row_head = jnp.arange(rows)[:, None] // (2 * dec_seq)
    own = lambda n_tok: (jnp.arange(n_tok * N_HEADS)[None, :] % N_HEADS) == row_head
    spread = lambda a: jnp.repeat(a, N_HEADS, axis=1)
    generic = jnp.where(own(page), 0.0, MASKED).astype(F32)
    last = jnp.where(own(page), spread(lastb), MASKED)
    pastb = jnp.concatenate([jnp.tile(generic, (1, n_pages - 1)), last], axis=1)
    new = jnp.where(own(dec_seq), spread(newb[:, :dec_seq]), MASKED)
    newb_full = jnp.concatenate(
        [new, jnp.full((rows, HEAD_W - dec_seq * N_HEADS), MASKED, F32)], axis=1)
    return pastb, newb_full


def _proj_out_ln_kernel(x_ref, zc_ref, za_ref, wc_ref, wa_ref, g_ref, b_ref, o_ref):
    mix = (jnp.dot(zc_ref[...].astype(BF16), wc_ref[...], preferred_element_type=F32)
           + jnp.dot(za_ref[...].astype(BF16), wa_ref[...], preferred_element_type=F32))
    o_ref[...] = _layer_norm(ALPHA * x_ref[...] + mix, g_ref[...], b_ref[...])


def _proj_out_ln(x, zc, za, w_conv, w_attn, g, b):
    t = x.shape[0]
    tm = min(TOKEN_TILE, t)
    row = lambda w: pl.BlockSpec((tm, w), lambda i: (i, 0))
    return pl.pallas_call(
        _proj_out_ln_kernel,
        grid=(t // tm,),
        in_specs=[row(D_MODEL), row(CONV_CH), row(ATTN_W), _resident(w_conv.shape),
                  _resident(w_attn.shape), _resident((1, D_MODEL)), _resident((1, D_MODEL))],
        out_specs=row(D_MODEL),
        out_shape=jax.ShapeDtypeStruct((t, D_MODEL), F32),
        compiler_params=_params(1),
        name="proj_out_ln",
    )(x, zc, za, w_conv, w_attn, g, b)


def kernel(x_prompt, x_sample, cache_k, cache_v, state_conv, page_table, rel_bias_table,
           w_in, w_out, conv_w, lambda_q1, lambda_k1, lambda_q2, lambda_k2, subln_w,
           ln_g, ln_b, ffn1_w_gate, ffn1_w_up, ffn1_w_down,
           ffn2_w_gate, ffn2_w_up, ffn2_w_down):
    batch, seq, _ = x_prompt.shape
    n_dec, dec_seq, _ = x_sample.shape
    depth, n_phys, page = cache_k.shape[:3]

    xp = x_prompt.reshape(batch * seq, D_MODEL)
    xs = x_sample.reshape(n_dec * dec_seq, D_MODEL)
    ck = cache_k.reshape(depth, n_phys, page * N_HEADS, HEAD_W)
    cv = cache_v.reshape(depth, n_phys, page * N_HEADS, HEAD_W)
    bf = lambda w: w.astype(BF16)

    band = jnp.stack([_bias_tiles(rel_bias_table, Q_TILE, 2 * Q_TILE, Q_TILE, off, 2 * Q_TILE)
                      for off in (0, Q_TILE)], axis=1)
    pastb, newb = _sample_bias_tiles(rel_bias_table, dec_seq, page, page_table.shape[1])

    outs = {name: [] for name in ("kp", "vp", "cp", "ks", "vs", "cs")}
    for l in range(depth):
        lam_init = 0.8 - 0.6 * math.exp(-0.3 * l)
        lam_p = jnp.stack([lambda_q1[l], lambda_k1[l], lambda_q2[l], lambda_k2[l]])
        subln = subln_w[l].reshape(1, V_DIM)
        g = lambda j: ln_g[l, j].reshape(1, D_MODEL)
        b = lambda j: ln_b[l, j].reshape(1, D_MODEL)
        f1 = (bf(ffn1_w_gate[l]), bf(ffn1_w_up[l]), bf(ffn1_w_down[l]))
        f2 = (bf(ffn2_w_gate[l]), bf(ffn2_w_up[l]), bf(ffn2_w_down[l]))
        wi = bf(w_in[l])
        wo_conv, wo_attn = bf(w_out[l, :CONV_CH]), bf(w_out[l, CONV_CH:])

        xp = _ffn_ln(xp, *f1, g(0), b(0))
        zc, q, k, v, kb, vb, cst = _proj_in_prompt(xp, wi, conv_w[l], batch, seq)
        za = _attn_prompt(q, kb, vb, band, lam_p, subln, lam_init, batch, seq)
        xp = _proj_out_ln(xp, zc, za, wo_conv, wo_attn, g(1), b(1))
        xp = _ffn_ln(xp, *f2, g(2), b(2))
        outs["kp"].append(k.reshape(batch, seq, N_HEADS, HEAD_W))
        outs["vp"].append(v.reshape(batch, seq, N_HEADS, V_DIM))
        outs["cp"].append(cst)

        xs = _ffn_ln(xs, *f1, g(0), b(0))
        zc, q, k, v, cst = _proj_in_sample(xs, wi, conv_w[l], state_conv[l], dec_seq)
        th = lambda a: a.reshape(n_dec, dec_seq * N_HEADS, HEAD_W)
        za = _attn_sample(q.reshape(n_dec, dec_seq, ATTN_W), th(k), th(v), ck, cv, l, page_table,
                          pastb, newb, lam_p, subln, lam_init)
        xs = _proj_out_ln(xs, zc, za.reshape(n_dec * dec_seq, ATTN_W), wo_conv, wo_attn, g(1), b(1))
        xs = _ffn_ln(xs, *f2, g(2), b(2))
        outs["ks"].append(k.reshape(n_dec, dec_seq, N_HEADS, HEAD_W))
        outs["vs"].append(v.reshape(n_dec, dec_seq, N_HEADS, V_DIM))
        outs["cs"].append(cst)

    return (xp.reshape(batch, seq, D_MODEL), xs.reshape(n_dec, dec_seq, D_MODEL),
            jnp.stack(outs["kp"]), jnp.stack(outs["vp"]), jnp.stack(outs["cp"]),
            jnp.stack(outs["ks"]), jnp.stack(outs["vs"]), jnp.stack(outs["cs"]))
```

```python
import functools
import math

import jax
import jax.numpy as jnp
from jax import lax
from jax.experimental import pallas as pl
from jax.experimental.pallas import tpu as pltpu

F32 = jnp.float32
BF16 = jnp.bfloat16

D_MODEL = 1024
DEPTH = 2
CONV_CH = D_MODEL // 2
CONV_K = 3
N_HEADS = 4
QK_DIM = 64
V_DIM = 2 * QK_DIM
HEAD_W = 2 * QK_DIM
ATTN_W = N_HEADS * V_DIM
D_FF = 2816
NUM_BUCKETS = 32
MAX_DISTANCE = 128
LN_EPS = 1e-5
ATTN_SCALE = QK_DIM ** -0.5
LOG2E = math.log2(math.e)
LOGIT_SCALE = ATTN_SCALE * LOG2E
ALPHA = (2 * DEPTH) ** 0.25
MASKED = -1e30

VMEM_LIMIT_BYTES = 56 * 1024 * 1024
TOKEN_TILE = 512
FF_CHUNK = 256
Q_TILE = 512
SUBLANES = 8


def _params(n_axes):
    return pltpu.CompilerParams(dimension_semantics=("arbitrary",) * n_axes,
                                vmem_limit_bytes=VMEM_LIMIT_BYTES)


def _resident(shape):
    return pl.BlockSpec(shape, lambda *_: (0,) * len(shape), pipeline_mode=pl.Buffered(1))


def _layer_norm(x, g, b):
    mu = jnp.mean(x, axis=-1, keepdims=True)
    xc = x - mu
    var = jnp.mean(xc * xc, axis=-1, keepdims=True)
    return xc * lax.rsqrt(var + LN_EPS) * g + b


def _ffn_ln_kernel(x_ref, wg_ref, wu_ref, wd_ref, g_ref, b_ref, o_ref, h_scr):
    x = x_ref[...]
    xb = x.astype(BF16)
    for c in range(D_FF // FF_CHUNK):
        sl = slice(c * FF_CHUNK, (c + 1) * FF_CHUNK)
        g = jnp.dot(xb, wg_ref[:, sl], preferred_element_type=F32)
        u = jnp.dot(xb, wu_ref[:, sl], preferred_element_type=F32)
        h_scr[:, sl] = (g * jax.nn.sigmoid(g) * u).astype(BF16)
    y = jnp.dot(h_scr[...], wd_ref[...], preferred_element_type=F32)
    o_ref[...] = _layer_norm(ALPHA * x + 0.5 * y, g_ref[...], b_ref[...])


def _ffn_ln(x, wg, wu, wd, g, b):
    t = x.shape[0]
    tm = min(TOKEN_TILE, t)
    row = pl.BlockSpec((tm, D_MODEL), lambda i: (i, 0))
    return pl.pallas_call(
        _ffn_ln_kernel,
        grid=(t // tm,),
        in_specs=[row, _resident((D_MODEL, D_FF)), _resident((D_MODEL, D_FF)),
                  _resident((D_FF, D_MODEL)), _resident((1, D_MODEL)), _resident((1, D_MODEL))],
        out_specs=row,
        out_shape=jax.ShapeDtypeStruct((t, D_MODEL), F32),
        scratch_shapes=[pltpu.VMEM((tm, D_FF), BF16)],
        compiler_params=_params(1),
        name="ffn_ln",
    )(x, wg, wu, wd, g, b)


def _section(xb, w_ref, c):
    return jnp.dot(xb, w_ref[:, c * CONV_CH:(c + 1) * CONV_CH], preferred_element_type=F32)


def _emit_qkv(xb, w_ref, q_ref, k_ref, v_ref, kb_ref, vb_ref):
    q_ref[...] = (_section(xb, w_ref, 3) * LOGIT_SCALE).astype(q_ref.dtype)
    k = _section(xb, w_ref, 4)
    k_ref[...] = k
    v = _section(xb, w_ref, 5)
    v_ref[...] = v
    if kb_ref is not None:
        kb_ref[...] = k.astype(BF16)
        vb_ref[...] = v.astype(BF16)


def _proj_in_prompt_kernel(x_ref, w_ref, cw_ref, zc_ref, q_ref, k_ref, v_ref, kb_ref, vb_ref,
                           cs_ref, u_scr, *, tiles_per_seq):
    tm = x_ref.shape[0]
    xb = x_ref[...].astype(BF16)
    bg = _section(xb, w_ref, 0)
    u = _section(xb, w_ref, 1) * _section(xb, w_ref, 2)

    @pl.when(pl.program_id(0) % tiles_per_seq == 0)
    def _():
        u_scr[0:SUBLANES, :] = jnp.zeros((SUBLANES, CONV_CH), F32)

    u_scr[SUBLANES:SUBLANES + tm, :] = u
    um1 = u_scr[SUBLANES - 1:SUBLANES - 1 + tm, :]
    um2 = u_scr[SUBLANES - 2:SUBLANES - 2 + tm, :]
    cw = cw_ref[...]
    y = cw[0:1] * um2 + cw[1:2] * um1 + cw[2:3] * u
    zc_ref[...] = (bg * y).astype(zc_ref.dtype)
    u_scr[0:SUBLANES, :] = u_scr[tm:tm + SUBLANES, :]
    cs_ref[0] = u[tm - (CONV_K - 1):tm, :]
    _emit_qkv(xb, w_ref, q_ref, k_ref, v_ref, kb_ref, vb_ref)


def _proj_in_prompt(x, w_in, conv_w, batch, seq):
    t = x.shape[0]
    tm = TOKEN_TILE
    tiles_per_seq = seq // tm
    row = lambda w: pl.BlockSpec((tm, w), lambda i: (i, 0))
    return pl.pallas_call(
        functools.partial(_proj_in_prompt_kernel, tiles_per_seq=tiles_per_seq),
        grid=(t // tm,),
        in_specs=[row(D_MODEL), _resident(w_in.shape), _resident(conv_w.shape)],
        out_specs=[row(CONV_CH), row(ATTN_W), row(ATTN_W), row(ATTN_W), row(ATTN_W), row(ATTN_W),
                   pl.BlockSpec((1, CONV_K - 1, CONV_CH), lambda i: (i // tiles_per_seq, 0, 0))],
        out_shape=[jax.ShapeDtypeStruct((t, CONV_CH), BF16),
                   jax.ShapeDtypeStruct((t, ATTN_W), BF16),
                   jax.ShapeDtypeStruct((t, ATTN_W), F32),
                   jax.ShapeDtypeStruct((t, ATTN_W), F32),
                   jax.ShapeDtypeStruct((t, ATTN_W), BF16),
                   jax.ShapeDtypeStruct((t, ATTN_W), BF16),
                   jax.ShapeDtypeStruct((batch, CONV_K - 1, CONV_CH), F32)],
        scratch_shapes=[pltpu.VMEM((tm + 2 * SUBLANES, CONV_CH), F32)],
        compiler_params=_params(1),
        name="proj_in_prompt",
    )(x, w_in, conv_w)


def _proj_in_sample_kernel(x_ref, w_ref, cw_ref, p1_ref, p2_ref, zc_ref, q_ref, k_ref, v_ref,
                           cs_ref, u_scr, *, dec_seq):
    tm = x_ref.shape[0]
    xb = x_ref[...].astype(BF16)
    bg = _section(xb, w_ref, 0)
    u = _section(xb, w_ref, 1) * _section(xb, w_ref, 2)
    u_scr[0:SUBLANES, :] = jnp.zeros((SUBLANES, CONV_CH), F32)
    u_scr[SUBLANES:SUBLANES + tm, :] = u
    tpos = lax.broadcasted_iota(jnp.int32, (tm, CONV_CH), 0) % dec_seq
    um1 = jnp.where(tpos >= 1, u_scr[SUBLANES - 1:SUBLANES - 1 + tm, :], p1_ref[...])
    um2 = jnp.where(tpos >= 2, u_scr[SUBLANES - 2:SUBLANES - 2 + tm, :], p2_ref[...])
    cw = cw_ref[...]
    y = cw[0:1] * um2 + cw[1:2] * um1 + cw[2:3] * u
    zc_ref[...] = (bg * y).astype(zc_ref.dtype)
    cs_ref[...] = u.reshape(tm // dec_seq, dec_seq, CONV_CH)[:, dec_seq - (CONV_K - 1):, :]
    _emit_qkv(xb, w_ref, q_ref, k_ref, v_ref, None, None)


def _proj_in_sample(x, w_in, conv_w, prev, dec_seq):
    t = x.shape[0]
    n_seq = t // dec_seq
    tm = min(TOKEN_TILE, t)
    p2 = jnp.pad(prev, ((0, 0), (0, dec_seq - (CONV_K - 1)), (0, 0))).reshape(t, CONV_CH)
    p1 = jnp.pad(prev[:, 1:], ((0, 0), (0, dec_seq - 1), (0, 0))).reshape(t, CONV_CH)
    row = lambda w: pl.BlockSpec((tm, w), lambda i: (i, 0))
    return pl.pallas_call(
        functools.partial(_proj_in_sample_kernel, dec_seq=dec_seq),
        grid=(t // tm,),
        in_specs=[row(D_MODEL), _resident(w_in.shape), _resident(conv_w.shape),
                  row(CONV_CH), row(CONV_CH)],
        out_specs=[row(CONV_CH), row(ATTN_W), row(ATTN_W), row(ATTN_W),
                   pl.BlockSpec((tm // dec_seq, CONV_K - 1, CONV_CH), lambda i: (i, 0, 0))],
        out_shape=[jax.ShapeDtypeStruct((t, CONV_CH), BF16),
                   jax.ShapeDtypeStruct((t, ATTN_W), F32),
                   jax.ShapeDtypeStruct((t, ATTN_W), F32),
                   jax.ShapeDtypeStruct((t, ATTN_W), F32),
                   jax.ShapeDtypeStruct((n_seq, CONV_K - 1, CONV_CH), F32)],
        scratch_shapes=[pltpu.VMEM((tm + 2 * SUBLANES, CONV_CH), F32)],
        compiler_params=_params(1),
        name="proj_in_sample",
    )(x, w_in, conv_w, p1, p2)


def _bias_kernel(tab_ref, o_ref, *, rmod, offset, cmax):
    h = pl.program_id(0)
    shape = o_ref.shape[1:]
    r = lax.broadcasted_iota(jnp.int32, shape, 0)
    c = lax.broadcasted_iota(jnp.int32, shape, 1)
    dist = (r % rmod) - c + offset
    n = jnp.maximum(dist, 0)
    max_exact = NUM_BUCKETS // 2
    nf = jnp.maximum(n, 1).astype(F32)
    large = max_exact + (jnp.log(nf / max_exact) / math.log(MAX_DISTANCE / max_exact)
                         * (NUM_BUCKETS - max_exact)).astype(jnp.int32)
    large = jnp.minimum(large, NUM_BUCKETS - 1)
    bucket = jnp.where(n < max_exact, n, large)
    last = tab_ref[NUM_BUCKETS - 1, h]
    bias = jnp.zeros(shape, F32)
    for b in range(NUM_BUCKETS - 1):
        bias = jnp.where(bucket == b, (tab_ref[b, h] - last) * LOG2E, bias)
    o_ref[0] = jnp.where((dist >= 0) & (c < cmax), bias, MASKED)


def _bias_tiles(table, rows, cols, rmod, offset, cmax):
    return pl.pallas_call(
        functools.partial(_bias_kernel, rmod=rmod, offset=offset, cmax=cmax),
        grid=(N_HEADS,),
        in_specs=[pl.BlockSpec(memory_space=pltpu.SMEM)],
        out_specs=pl.BlockSpec((1, rows, cols), lambda h: (h, 0, 0)),
        out_shape=jax.ShapeDtypeStruct((N_HEADS, rows, cols), F32),
        compiler_params=_params(1),
        name="rel_bias_tiles",
    )(table)


def _lambda_full(lam_ref, lam_init):
    lp = lam_ref[...]
    d1 = jnp.sum(lp[0:1] * lp[1:2], axis=-1, keepdims=True)
    d2 = jnp.sum(lp[2:3] * lp[3:4], axis=-1, keepdims=True)
    return jnp.exp(d1) - jnp.exp(d2) + lam_init


def _head_out(o1, o2, lam, subln, lam_init):
    od = o1 - lam * o2
    ms = jnp.mean(od * od, axis=-1, keepdims=True)
    return od * lax.rsqrt(ms + LN_EPS) * subln * (1.0 - lam_init)


M_INIT = -1e29


def _attn_prompt_kernel(lam_ref, subln_ref, q_ref, k_ref, v_ref, band_ref, o_ref,
                        qs_scr, s0, s1, p0, p1, a0, a1, m_scr, l_scr, acc_scr, *, lam_init):
    tq = Q_TILE
    rows = 2 * tq
    nq = q_ref.shape[0] // tq
    n_pairs = nq * (nq + 1) // 2
    assert (n_pairs + 2) % 2 == 0
    lane = lax.broadcasted_iota(jnp.int32, (tq, HEAD_W), 1)

    s0[...] = jnp.full(s0.shape, MASKED, F32)
    s1[...] = jnp.full(s1.shape, MASKED, F32)
    p0[...] = jnp.zeros(p0.shape, BF16)
    p1[...] = jnp.zeros(p1.shape, BF16)
    a0[...] = jnp.ones(a0.shape, F32)
    a1[...] = jnp.ones(a1.shape, F32)
    m_scr[...] = jnp.full(m_scr.shape, M_INIT, F32)
    l_scr[...] = jnp.zeros(l_scr.shape, F32)
    acc_scr[...] = jnp.zeros(acc_scr.shape, F32)
    qs_scr[...] = jnp.zeros(qs_scr.shape, BF16)

    def key_rows(j):
        return pl.ds(pl.multiple_of(j * tq, tq), tq)

    def half_step(pair_a, pair_b, pair_c, s_a, s_b, p_b, a_b, p_c, a_c):
        ia, ja = pair_a
        ib, jb = pair_b
        ic, jc = pair_c

        @pl.when((ja == 0) & (ia < nq))
        def _():
            q = q_ref[pl.ds(pl.multiple_of(ia * tq, tq), tq), :]
            zero = jnp.zeros_like(q)
            qs_scr[ia % 2, 0:tq, :] = jnp.where(lane < QK_DIM, q, zero)
            qs_scr[ia % 2, tq:rows, :] = jnp.where(lane >= QK_DIM, q, zero)

        @pl.when(ib >= nq)
        def _():
            s_b[...] = jnp.full(s_b.shape, MASKED, F32)

        for kind in range(2):
            @pl.when((ib < nq) & (jb == ib - 1 + kind))
            def _():
                s_b[0:tq, :] = s_b[0:tq, :] + band_ref[0, kind]
                s_b[tq:rows, :] = s_b[tq:rows, :] + band_ref[0, kind]

        @pl.when(jb == 0)
        def _():
            m_scr[ib % 2] = jnp.full(m_scr.shape[1:], M_INIT, F32)
            l_scr[ib % 2] = jnp.zeros(l_scr.shape[1:], F32)

        @pl.when(jc == 0)
        def _():
            acc_scr[...] = jnp.zeros(acc_scr.shape, F32)

        s_a[...] = lax.dot_general(qs_scr[ia % 2], k_ref[key_rows(ja), :], (((1,), (1,)), ((), ())),
                                   preferred_element_type=F32)
        par = ib % 2
        s = s_b[...]
        m_prev = m_scr[par]
        m_new = jnp.maximum(m_prev, jnp.max(s, axis=-1, keepdims=True))
        alpha = jnp.exp2(m_prev - m_new)
        p = jnp.exp2(s - jnp.concatenate([m_new] * (tq // HEAD_W), axis=1))
        l_scr[par] = alpha * l_scr[par] + jnp.sum(p, axis=-1, keepdims=True)
        m_scr[par] = m_new
        p_b[...] = p.astype(BF16)
        a_b[...] = alpha
        acc_scr[...] = a_c[...] * acc_scr[...] + jnp.dot(p_c[...], v_ref[key_rows(jc), :],
                                                         preferred_element_type=F32)

        @pl.when((jc == ic) & (ic < nq))
        def _():
            o = acc_scr[...] / l_scr[ic % 2]
            z = _head_out(o[0:tq], o[tq:rows], _lambda_full(lam_ref, lam_init), subln_ref[...],
                          lam_init)
            o_ref[pl.ds(pl.multiple_of(ic * tq, tq), tq), :] = z.astype(o_ref.dtype)

    def advance(pair):
        i, j = pair
        wrap = j >= i
        return jnp.where(wrap, i + 1, i), jnp.where(wrap, 0, j + 1)

    def body(_, carry):
        pair_a, pair_b, pair_c = carry
        half_step(pair_a, pair_b, pair_c, s0, s1, p1, a1, p0, a0)
        pair_a, pair_b, pair_c = advance(pair_a), pair_a, pair_b
        half_step(pair_a, pair_b, pair_c, s1, s0, p0, a0, p1, a1)
        return advance(pair_a), pair_a, pair_b

    idle = (jnp.int32(0), jnp.int32(1))
    lax.fori_loop(0, (n_pairs + 2) // 2, body, ((jnp.int32(0), jnp.int32(0)), idle, idle))


def _attn_prompt(q, kb, vb, band, lam_p, subln, lam_init, batch, seq):
    t = q.shape[0]
    tq = Q_TILE
    rows = 2 * tq
    seq_spec = pl.BlockSpec((seq, HEAD_W), lambda b, h: (b, h))
    return pl.pallas_call(
        functools.partial(_attn_prompt_kernel, lam_init=lam_init),
        grid=(batch, N_HEADS),
        in_specs=[pl.BlockSpec((4, QK_DIM), lambda b, h: (0, 0)),
                  pl.BlockSpec((1, V_DIM), lambda b, h: (0, 0)),
                  seq_spec, seq_spec, seq_spec,
                  pl.BlockSpec((1, 2, tq, tq), lambda b, h: (h, 0, 0, 0))],
        out_specs=seq_spec,
        out_shape=jax.ShapeDtypeStruct((t, ATTN_W), BF16),
        scratch_shapes=[pltpu.VMEM((2, rows, HEAD_W), BF16),
                        pltpu.VMEM((rows, tq), F32), pltpu.VMEM((rows, tq), F32),
                        pltpu.VMEM((rows, tq), BF16), pltpu.VMEM((rows, tq), BF16),
                        pltpu.VMEM((rows, V_DIM), F32), pltpu.VMEM((rows, V_DIM), F32),
                        pltpu.VMEM((2, rows, V_DIM), F32), pltpu.VMEM((2, rows, V_DIM), F32),
                        pltpu.VMEM((rows, V_DIM), F32)],
        compiler_params=_params(2),
        name="attn_prompt",
    )(lam_p, subln, q, kb, vb, band)


def _page_copies(pt_ref, ck_ref, cv_ref, kbuf, vbuf, sem, seq, slot, layer, n_pages):
    rows = ck_ref.shape[2]
    copies = []
    for p in range(n_pages):
        phys = 0 if seq is None else pt_ref[seq * n_pages + p]
        dst = pl.ds(p * rows, rows)
        copies.append(pltpu.make_async_copy(ck_ref.at[layer, phys], kbuf.at[slot, dst], sem.at[slot, 0]))
        copies.append(pltpu.make_async_copy(cv_ref.at[layer, phys], vbuf.at[slot, dst], sem.at[slot, 1]))
    return copies


def _attn_sample_kernel(pt_ref, lam_ref, subln_ref, q_ref, kn_ref, vn_ref, pastb_ref, newb_ref,
                        ck_ref, cv_ref, o_ref, kbuf, vbuf, kn_scr, vn_scr, sem,
                        *, lam_init, layer, n_pages):
    seq = pl.program_id(0)
    slot = seq % 2
    copies = functools.partial(_page_copies, pt_ref, ck_ref, cv_ref, kbuf, vbuf, sem,
                               layer=layer, n_pages=n_pages)

    @pl.when(seq == 0)
    def _():
        for c in copies(seq=0, slot=0):
            c.start()

    @pl.when(seq + 1 < pl.num_programs(0))
    def _():
        for c in copies(seq=seq + 1, slot=1 - slot):
            c.start()

    dec_seq = q_ref.shape[1]
    q = q_ref[0]
    lane = lax.broadcasted_iota(jnp.int32, (dec_seq, HEAD_W), 1)
    blocks = []
    for h in range(N_HEADS):
        qh = q[:, h * HEAD_W:(h + 1) * HEAD_W]
        blocks += [jnp.where(lane < QK_DIM, qh, 0.0), jnp.where(lane >= QK_DIM, qh, 0.0)]
    qall = jnp.concatenate(blocks, axis=0).astype(BF16)

    def logits(k):
        return lax.dot_general(qall, k, (((1,), (1,)), ((), ())), preferred_element_type=F32)

    n_new = kn_ref.shape[1]
    kn_scr[...] = jnp.zeros(kn_scr.shape, F32)
    vn_scr[...] = jnp.zeros(vn_scr.shape, F32)
    kn_scr[0:n_new, :] = kn_ref[0]
    vn_scr[0:n_new, :] = vn_ref[0]
    s_new = logits(kn_scr[...].astype(BF16)) + newb_ref[...]

    for c in copies(seq=None, slot=slot):
        c.wait()
    s_past = logits(kbuf[slot].astype(BF16)) + pastb_ref[...]
    m = jnp.maximum(jnp.max(s_past, axis=-1, keepdims=True), jnp.max(s_new, axis=-1, keepdims=True))
    p_past = jnp.exp2(s_past - m)
    p_new = jnp.exp2(s_new - m)
    l = jnp.sum(p_past, axis=-1, keepdims=True) + jnp.sum(p_new, axis=-1, keepdims=True)
    acc = (jnp.dot(p_past.astype(BF16), vbuf[slot].astype(BF16), preferred_element_type=F32)
           + jnp.dot(p_new.astype(BF16), vn_scr[...].astype(BF16), preferred_element_type=F32))
    o = acc / l
    lam = _lambda_full(lam_ref, lam_init)
    heads = []
    for h in range(N_HEADS):
        oh = o[2 * dec_seq * h:2 * dec_seq * (h + 1)]
        heads.append(_head_out(oh[0:dec_seq], oh[dec_seq:2 * dec_seq], lam, subln_ref[...], lam_init))
    o_ref[0] = jnp.concatenate(heads, axis=-1)


def _attn_sample(q, k_new, v_new, cache_k, cache_v, layer, page_table, pastb, newb, lam_p, subln,
                 lam_init):
    n_seq, dec_seq, _ = q.shape
    n_pages = page_table.shape[1]
    page_rows = cache_k.shape[2]
    rows = N_HEADS * 2 * dec_seq
    past_rows = n_pages * page_rows
    per_seq = lambda a: pl.BlockSpec((1,) + a.shape[1:], lambda s, pt: (s, 0, 0))
    const = lambda shape: pl.BlockSpec(shape, lambda s, pt: (0,) * len(shape),
                                       pipeline_mode=pl.Buffered(1))
    hbm = pl.BlockSpec(memory_space=pl.ANY)
    grid_spec = pltpu.PrefetchScalarGridSpec(
        num_scalar_prefetch=1,
        grid=(n_seq,),
        in_specs=[const((4, QK_DIM)), const((1, V_DIM)), per_seq(q), per_seq(k_new), per_seq(v_new),
                  const((rows, past_rows)), const((rows, HEAD_W)), hbm, hbm],
        out_specs=per_seq(q),
        scratch_shapes=[pltpu.VMEM((2, past_rows, HEAD_W), F32), pltpu.VMEM((2, past_rows, HEAD_W), F32),
                        pltpu.VMEM((HEAD_W, HEAD_W), F32), pltpu.VMEM((HEAD_W, HEAD_W), F32),
                        pltpu.SemaphoreType.DMA((2, 2))],
    )
    return pl.pallas_call(
        functools.partial(_attn_sample_kernel, lam_init=lam_init, layer=layer, n_pages=n_pages),
        grid_spec=grid_spec,
        out_shape=jax.ShapeDtypeStruct((n_seq, dec_seq, ATTN_W), F32),
        compiler_params=_params(1),
        name="attn_sample",
    )(page_table.reshape(-1), lam_p, subln, q, k_new, v_new, pastb, newb, cache_k, cache_v)


def _sample_bias_tiles(table, dec_seq, page, n_pages):
    rows = N_HEADS * 2 * dec_seq
    lastb = _bias_tiles(table, 2 * dec_seq, page, dec_seq, page, page).reshape(rows, page)
    newb = _bias_tiles(table, 2 * dec_seq, page, dec_seq, 0, dec_seq).reshape(rows, page)
    row_head = jnp.arange(rows)[:, None] // (2 * dec_seq)
    own = lambda n_tok: (jnp.arange(n_tok * N_HEADS)[None, :] % N_HEADS) == row_head
    spread = lambda a: jnp.repeat(a, N_HEADS, axis=1)
    generic = jnp.where(own(page), 0.0, MASKED).astype(F32)
    last = jnp.where(own(page), spread(lastb), MASKED)
    pastb = jnp.concatenate([jnp.tile(generic, (1, n_pages - 1)), last], axis=1)
    new = jnp.where(own(dec_seq), spread(newb[:, :dec_seq]), MASKED)
    newb_full = jnp.concatenate(
        [new, jnp.full((rows, HEAD_W - dec_seq * N_HEADS), MASKED, F32)], axis=1)
    return pastb, newb_full


def _proj_out_ln_kernel(x_ref, zc_ref, za_ref, wc_ref, wa_ref, g_ref, b_ref, o_ref):
    mix = (jnp.dot(zc_ref[...].astype(BF16), wc_ref[...], preferred_element_type=F32)
           + jnp.dot(za_ref[...].astype(BF16), wa_ref[...], preferred_element_type=F32))
    o_ref[...] = _layer_norm(ALPHA * x_ref[...] + mix, g_ref[...], b_ref[...])


def _proj_out_ln(x, zc, za, w_conv, w_attn, g, b):
    t = x.shape[0]
    tm = min(TOKEN_TILE, t)
    row = lambda w: pl.BlockSpec((tm, w), lambda i: (i, 0))
    return pl.pallas_call(
        _proj_out_ln_kernel,
        grid=(t // tm,),
        in_specs=[row(D_MODEL), row(CONV_CH), row(ATTN_W), _resident(w_conv.shape),
                  _resident(w_attn.shape), _resident((1, D_MODEL)), _resident((1, D_MODEL))],
        out_specs=row(D_MODEL),
        out_shape=jax.ShapeDtypeStruct((t, D_MODEL), F32),
        compiler_params=_params(1),
        name="proj_out_ln",
    )(x, zc, za, w_conv, w_attn, g, b)


def kernel(x_prompt, x_sample, cache_k, cache_v, state_conv, page_table, rel_bias_table,
           w_in, w_out, conv_w, lambda_q1, lambda_k1, lambda_q2, lambda_k2, subln_w,
           ln_g, ln_b, ffn1_w_gate, ffn1_w_up, ffn1_w_down,
           ffn2_w_gate, ffn2_w_up, ffn2_w_down):
    batch, seq, _ = x_prompt.shape
    n_dec, dec_seq, _ = x_sample.shape
    depth, n_phys, page = cache_k.shape[:3]

    xp = x_prompt.reshape(batch * seq, D_MODEL)
    xs = x_sample.reshape(n_dec * dec_seq, D_MODEL)
    ck = cache_k.reshape(depth, n_phys, page * N_HEADS, HEAD_W)
    cv = cache_v.reshape(depth, n_phys, page * N_HEADS, HEAD_W)
    bf = lambda w: w.astype(BF16)

    band = jnp.stack([_bias_tiles(rel_bias_table, Q_TILE, Q_TILE, Q_TILE, off, Q_TILE)
                      for off in (Q_TILE, 0)], axis=1)
    pastb, newb = _sample_bias_tiles(rel_bias_table, dec_seq, page, page_table.shape[1])

    outs = {name: [] for name in ("kp", "vp", "cp", "ks", "vs", "cs")}
    for l in range(depth):
        lam_init = 0.8 - 0.6 * math.exp(-0.3 * l)
        lam_p = jnp.stack([lambda_q1[l], lambda_k1[l], lambda_q2[l], lambda_k2[l]])
        subln = subln_w[l].reshape(1, V_DIM)
        g = lambda j: ln_g[l, j].reshape(1, D_MODEL)
        b = lambda j: ln_b[l, j].reshape(1, D_MODEL)
        f1 = (bf(ffn1_w_gate[l]), bf(ffn1_w_up[l]), bf(ffn1_w_down[l]))
        f2 = (bf(ffn2_w_gate[l]), bf(ffn2_w_up[l]), bf(ffn2_w_down[l]))
        wi = bf(w_in[l])
        wo_conv, wo_attn = bf(w_out[l, :CONV_CH]), bf(w_out[l, CONV_CH:])

        xp = _ffn_ln(xp, *f1, g(0), b(0))
        zc, q, k, v, kb, vb, cst = _proj_in_prompt(xp, wi, conv_w[l], batch, seq)
        za = _attn_prompt(q, kb, vb, band, lam_p, subln, lam_init, batch, seq)
        xp = _proj_out_ln(xp, zc, za, wo_conv, wo_attn, g(1), b(1))
        xp = _ffn_ln(xp, *f2, g(2), b(2))
        outs["kp"].append(k.reshape(batch, seq, N_HEADS, HEAD_W))
        outs["vp"].append(v.reshape(batch, seq, N_HEADS, V_DIM))
        outs["cp"].append(cst)

        xs = _ffn_ln(xs, *f1, g(0), b(0))
        zc, q, k, v, cst = _proj_in_sample(xs, wi, conv_w[l], state_conv[l], dec_seq)
        th = lambda a: a.reshape(n_dec, dec_seq * N_HEADS, HEAD_W)
        za = _attn_sample(q.reshape(n_dec, dec_seq, ATTN_W), th(k), th(v), ck, cv, l, page_table,
                          pastb, newb, lam_p, subln, lam_init)
        xs = _proj_out_ln(xs, zc, za.reshape(n_dec * dec_seq, ATTN_W), wo_conv, wo_attn, g(1), b(1))
        xs = _ffn_ln(xs, *f2, g(2), b(2))
        outs["ks"].append(k.reshape(n_dec, dec_seq, N_HEADS, HEAD_W))
        outs["vs"].append(v.reshape(n_dec, dec_seq, N_HEADS, V_DIM))
        outs["cs"].append(cst)

    return (xp.reshape(batch, seq, D_MODEL), xs.reshape(n_dec, dec_seq, D_MODEL),
            jnp.stack(outs["kp"]), jnp.stack(outs["vp"]), jnp.stack(outs["cp"]),
            jnp.stack(outs["ks"]), jnp.stack(outs["vs"]), jnp.stack(outs["cs"]))
```

```python
import functools
import math

import jax
import jax.numpy as jnp
from jax import lax
from jax.experimental import pallas as pl
from jax.experimental.pallas import tpu as pltpu

F32 = jnp.float32
BF16 = jnp.bfloat16

D_MODEL = 1024
DEPTH = 2
CONV_CH = D_MODEL // 2
CONV_K = 3
N_HEADS = 4
QK_DIM = 64
V_DIM = 2 * QK_DIM
HEAD_W = 2 * QK_DIM
ATTN_W = N_HEADS * V_DIM
D_FF = 2816
NUM_BUCKETS = 32
MAX_DISTANCE = 128
LN_EPS = 1e-5
ATTN_SCALE = QK_DIM ** -0.5
LOG2E = math.log2(math.e)
LOGIT_SCALE = ATTN_SCALE * LOG2E
ALPHA = (2 * DEPTH) ** 0.25
MASKED = -1e30

VMEM_LIMIT_BYTES = 56 * 1024 * 1024
TOKEN_TILE = 512
FF_CHUNK = 256
Q_TILE = 512
K_TILE = 512
ROW_CHUNK = 256
SUBLANES = 8


def _params(n_axes):
    return pltpu.CompilerParams(dimension_semantics=("arbitrary",) * n_axes,
                                vmem_limit_bytes=VMEM_LIMIT_BYTES)


def _resident(shape):
    return pl.BlockSpec(shape, lambda *_: (0,) * len(shape), pipeline_mode=pl.Buffered(1))


def _layer_resident(stacked, layer, block_rows=None, row_block=0):
    _, rows, cols = stacked.shape
    return pl.BlockSpec((None, block_rows or rows, cols), lambda *_: (layer, row_block, 0),
                        pipeline_mode=pl.Buffered(1))


def _layer_norm(x, g, b):
    mu = jnp.mean(x, axis=-1, keepdims=True)
    xc = x - mu
    var = jnp.mean(xc * xc, axis=-1, keepdims=True)
    return xc * lax.rsqrt(var + LN_EPS) * g + b


def _ffn_ln_kernel(x_ref, wg_ref, wu_ref, wd_ref, g_ref, b_ref, o_ref, h_scr):
    x = x_ref[...]
    xb = x.astype(BF16)
    for c in range(D_FF // FF_CHUNK):
        sl = slice(c * FF_CHUNK, (c + 1) * FF_CHUNK)
        g = jnp.dot(xb, wg_ref[:, sl], preferred_element_type=F32)
        u = jnp.dot(xb, wu_ref[:, sl], preferred_element_type=F32)
        h_scr[:, sl] = (g * jax.nn.sigmoid(g) * u).astype(BF16)
    y = jnp.dot(h_scr[...], wd_ref[...], preferred_element_type=F32)
    o_ref[...] = _layer_norm(ALPHA * x + 0.5 * y, g_ref[...], b_ref[...])


def _ffn_ln(x, wg, wu, wd, layer, g, b):
    t = x.shape[0]
    tm = min(TOKEN_TILE, t)
    row = pl.BlockSpec((tm, D_MODEL), lambda i: (i, 0))
    return pl.pallas_call(
        _ffn_ln_kernel,
        grid=(t // tm,),
        in_specs=[row, _layer_resident(wg, layer), _layer_resident(wu, layer),
                  _layer_resident(wd, layer), _resident((1, D_MODEL)), _resident((1, D_MODEL))],
        out_specs=row,
        out_shape=jax.ShapeDtypeStruct((t, D_MODEL), F32),
        scratch_shapes=[pltpu.VMEM((tm, D_FF), BF16)],
        compiler_params=_params(1),
        name="ffn_ln",
    )(x, wg, wu, wd, g, b)


def _section(xb, w_ref, c):
    return jnp.dot(xb, w_ref[:, c * CONV_CH:(c + 1) * CONV_CH], preferred_element_type=F32)


def _store_head_rows(ref, x):
    tokens = x.shape[0]
    for h in range(N_HEADS):
        ref[pl.ds(h, tokens, stride=N_HEADS), :] = x[:, h * HEAD_W:(h + 1) * HEAD_W]


def _emit_qkv(xb, w_ref, q_ref, k_ref, v_ref, kb_ref, vb_ref):
    q_ref[...] = (_section(xb, w_ref, 3) * LOGIT_SCALE).astype(q_ref.dtype)
    k = _section(xb, w_ref, 4)
    _store_head_rows(k_ref, k)
    v = _section(xb, w_ref, 5)
    _store_head_rows(v_ref, v)
    if kb_ref is not None:
        kb_ref[...] = k.astype(BF16)
        vb_ref[...] = v.astype(BF16)


def _proj_in_prompt_kernel(x_ref, w_ref, cw_ref, k_all_ref, v_all_ref, zc_ref, q_ref, k_ref, v_ref,
                           kb_ref, vb_ref, cs_ref, u_scr, *, tiles_per_seq):
    del k_all_ref, v_all_ref
    tm = x_ref.shape[0]
    xb = x_ref[...].astype(BF16)
    bg = _section(xb, w_ref, 0)
    u = _section(xb, w_ref, 1) * _section(xb, w_ref, 2)

    @pl.when(pl.program_id(0) % tiles_per_seq == 0)
    def _():
        u_scr[0:SUBLANES, :] = jnp.zeros((SUBLANES, CONV_CH), F32)

    u_scr[SUBLANES:SUBLANES + tm, :] = u
    um1 = u_scr[SUBLANES - 1:SUBLANES - 1 + tm, :]
    um2 = u_scr[SUBLANES - 2:SUBLANES - 2 + tm, :]
    cw = cw_ref[...]
    y = cw[0:1] * um2 + cw[1:2] * um1 + cw[2:3] * u
    zc_ref[...] = (bg * y).astype(zc_ref.dtype)
    u_scr[0:SUBLANES, :] = u_scr[tm:tm + SUBLANES, :]
    cs_ref[0] = u[tm - (CONV_K - 1):tm, :]
    _emit_qkv(xb, w_ref, q_ref, k_ref, v_ref, kb_ref, vb_ref)


def _head_rows_spec(tm, n_tiles, layer):
    return pl.BlockSpec((tm * N_HEADS, HEAD_W), lambda i: (layer * n_tiles + i, 0))


def _proj_in_prompt(x, w_in, conv_w, k_all, v_all, layer, batch, seq):
    t = x.shape[0]
    tm = TOKEN_TILE
    tiles_per_seq = seq // tm
    row = lambda w: pl.BlockSpec((tm, w), lambda i: (i, 0))
    hbm = pl.BlockSpec(memory_space=pl.ANY)
    head_rows = _head_rows_spec(tm, t // tm, layer)
    same = lambda a: jax.ShapeDtypeStruct(a.shape, a.dtype)
    return pl.pallas_call(
        functools.partial(_proj_in_prompt_kernel, tiles_per_seq=tiles_per_seq),
        grid=(t // tm,),
        in_specs=[row(D_MODEL), _layer_resident(w_in, layer), _layer_resident(conv_w, layer), hbm, hbm],
        out_specs=[row(CONV_CH), row(ATTN_W), head_rows, head_rows, row(ATTN_W), row(ATTN_W),
                   pl.BlockSpec((1, CONV_K - 1, CONV_CH), lambda i: (i // tiles_per_seq, 0, 0))],
        out_shape=[jax.ShapeDtypeStruct((t, CONV_CH), BF16),
                   jax.ShapeDtypeStruct((t, ATTN_W), BF16),
                   same(k_all), same(v_all),
                   jax.ShapeDtypeStruct((t, ATTN_W), BF16),
                   jax.ShapeDtypeStruct((t, ATTN_W), BF16),
                   jax.ShapeDtypeStruct((batch, CONV_K - 1, CONV_CH), F32)],
        input_output_aliases={3: 2, 4: 3},
        scratch_shapes=[pltpu.VMEM((tm + 2 * SUBLANES, CONV_CH), F32)],
        compiler_params=_params(1),
        name="proj_in_prompt",
    )(x, w_in, conv_w, k_all, v_all)


def _proj_in_sample_kernel(x_ref, w_ref, cw_ref, p1_ref, p2_ref, k_all_ref, v_all_ref, zc_ref, q_ref,
                           k_ref, v_ref, cs_ref, u_scr, *, dec_seq):
    del k_all_ref, v_all_ref
    tm = x_ref.shape[0]
    xb = x_ref[...].astype(BF16)
    bg = _section(xb, w_ref, 0)
    u = _section(xb, w_ref, 1) * _section(xb, w_ref, 2)
    u_scr[0:SUBLANES, :] = jnp.zeros((SUBLANES, CONV_CH), F32)
    u_scr[SUBLANES:SUBLANES + tm, :] = u
    tpos = lax.broadcasted_iota(jnp.int32, (tm, CONV_CH), 0) % dec_seq
    um1 = jnp.where(tpos >= 1, u_scr[SUBLANES - 1:SUBLANES - 1 + tm, :], p1_ref[...])
    um2 = jnp.where(tpos >= 2, u_scr[SUBLANES - 2:SUBLANES - 2 + tm, :], p2_ref[...])
    cw = cw_ref[...]
    y = cw[0:1] * um2 + cw[1:2] * um1 + cw[2:3] * u
    zc_ref[...] = (bg * y).astype(zc_ref.dtype)
    cs_ref[...] = u.reshape(tm // dec_seq, dec_seq, CONV_CH)[:, dec_seq - (CONV_K - 1):, :]
    _emit_qkv(xb, w_ref, q_ref, k_ref, v_ref, None, None)


def _proj_in_sample(x, w_in, conv_w, k_all, v_all, layer, prev, dec_seq):
    t = x.shape[0]
    n_seq = t // dec_seq
    tm = min(TOKEN_TILE, t)
    p2 = jnp.pad(prev, ((0, 0), (0, dec_seq - (CONV_K - 1)), (0, 0))).reshape(t, CONV_CH)
    p1 = jnp.pad(prev[:, 1:], ((0, 0), (0, dec_seq - 1), (0, 0))).reshape(t, CONV_CH)
    row = lambda w: pl.BlockSpec((tm, w), lambda i: (i, 0))
    hbm = pl.BlockSpec(memory_space=pl.ANY)
    head_rows = _head_rows_spec(tm, t // tm, layer)
    same = lambda a: jax.ShapeDtypeStruct(a.shape, a.dtype)
    return pl.pallas_call(
        functools.partial(_proj_in_sample_kernel, dec_seq=dec_seq),
        grid=(t // tm,),
        in_specs=[row(D_MODEL), _layer_resident(w_in, layer), _layer_resident(conv_w, layer),
                  row(CONV_CH), row(CONV_CH), hbm, hbm],
        out_specs=[row(CONV_CH), row(ATTN_W), head_rows, head_rows,
                   pl.BlockSpec((tm // dec_seq, CONV_K - 1, CONV_CH), lambda i: (i, 0, 0))],
        out_shape=[jax.ShapeDtypeStruct((t, CONV_CH), BF16),
                   jax.ShapeDtypeStruct((t, ATTN_W), F32),
                   same(k_all), same(v_all),
                   jax.ShapeDtypeStruct((n_seq, CONV_K - 1, CONV_CH), F32)],
        input_output_aliases={5: 2, 6: 3},
        scratch_shapes=[pltpu.VMEM((tm + 2 * SUBLANES, CONV_CH), F32)],
        compiler_params=_params(1),
        name="proj_in_sample",
    )(x, w_in, conv_w, p1, p2, k_all, v_all)


def _bias_kernel(tab_ref, o_ref, *, rmod, offset, cmax):
    h = pl.program_id(0)
    shape = o_ref.shape[1:]
    r = lax.broadcasted_iota(jnp.int32, shape, 0)
    c = lax.broadcasted_iota(jnp.int32, shape, 1)
    dist = (r % rmod) - c + offset
    n = jnp.maximum(dist, 0)
    max_exact = NUM_BUCKETS // 2
    nf = jnp.maximum(n, 1).astype(F32)
    large = max_exact + (jnp.log(nf / max_exact) / math.log(MAX_DISTANCE / max_exact)
                         * (NUM_BUCKETS - max_exact)).astype(jnp.int32)
    large = jnp.minimum(large, NUM_BUCKETS - 1)
    bucket = jnp.where(n < max_exact, n, large)
    last = tab_ref[NUM_BUCKETS - 1, h]
    bias = jnp.zeros(shape, F32)
    for b in range(NUM_BUCKETS - 1):
        bias = jnp.where(bucket == b, (tab_ref[b, h] - last) * LOG2E, bias)
    o_ref[0] = jnp.where((dist >= 0) & (c < cmax), bias, MASKED)


def _bias_tiles(table, rows, cols, rmod, offset, cmax):
    return pl.pallas_call(
        functools.partial(_bias_kernel, rmod=rmod, offset=offset, cmax=cmax),
        grid=(N_HEADS,),
        in_specs=[pl.BlockSpec(memory_space=pltpu.SMEM)],
        out_specs=pl.BlockSpec((1, rows, cols), lambda h: (h, 0, 0)),
        out_shape=jax.ShapeDtypeStruct((N_HEADS, rows, cols), F32),
        compiler_params=_params(1),
        name="rel_bias_tiles",
    )(table)


def _lambda_full(lam_ref, lam_init):
    lp = lam_ref[...]
    d1 = jnp.sum(lp[0:1] * lp[1:2], axis=-1, keepdims=True)
    d2 = jnp.sum(lp[2:3] * lp[3:4], axis=-1, keepdims=True)
    return jnp.exp(d1) - jnp.exp(d2) + lam_init


def _head_out(o1, o2, lam, subln, lam_init):
    od = o1 - lam * o2
    ms = jnp.mean(od * od, axis=-1, keepdims=True)
    return od * lax.rsqrt(ms + LN_EPS) * subln * (1.0 - lam_init)


M_INIT = -1e29


def _attn_prompt_kernel(lam_ref, subln_ref, q_ref, k_ref, v_ref, band_ref, o_ref,
                        qs_scr, s0, s1, p0, p1, a0, a1, m_scr, l_scr, acc_scr, *, lam_init):
    tq, tk = Q_TILE, K_TILE
    ratio = tk // tq
    rows = 2 * tq
    nq = q_ref.shape[0] // tq
    last_key_tile = lambda i: i // ratio
    n_pairs = sum(last_key_tile(i) + 1 for i in range(nq))
    assert (n_pairs + 2) % 2 == 0
    lane = lax.broadcasted_iota(jnp.int32, (tq, HEAD_W), 1)

    s0[...] = jnp.full(s0.shape, MASKED, F32)
    s1[...] = jnp.full(s1.shape, MASKED, F32)
    p0[...] = jnp.zeros(p0.shape, BF16)
    p1[...] = jnp.zeros(p1.shape, BF16)
    a0[...] = jnp.ones(a0.shape, F32)
    a1[...] = jnp.ones(a1.shape, F32)
    m_scr[...] = jnp.full(m_scr.shape, M_INIT, F32)
    l_scr[...] = jnp.zeros(l_scr.shape, F32)
    acc_scr[...] = jnp.zeros(acc_scr.shape, F32)
    qs_scr[...] = jnp.zeros(qs_scr.shape, BF16)

    def key_rows(j):
        return pl.ds(pl.multiple_of(j * tk, tk), tk)

    def half_step(pair_a, pair_b, pair_c, s_a, s_b, p_b, a_b, p_c, a_c):
        ia, ja = pair_a
        ib, jb = pair_b
        ic, jc = pair_c

        @pl.when((ja == 0) & (ia < nq))
        def _():
            q = q_ref[pl.ds(pl.multiple_of(ia * tq, tq), tq), :]
            zero = jnp.zeros_like(q)
            qs_scr[ia % 2, 0:tq, :] = jnp.where(lane < QK_DIM, q, zero)
            qs_scr[ia % 2, tq:rows, :] = jnp.where(lane >= QK_DIM, q, zero)

        @pl.when(ib >= nq)
        def _():
            s_b[...] = jnp.full(s_b.shape, MASKED, F32)

        for kind in range(ratio + 1):
            @pl.when((ib < nq) & (ib - jb * ratio == kind))
            def _():
                s_b[0:tq, :] = s_b[0:tq, :] + band_ref[0, kind]
                s_b[tq:rows, :] = s_b[tq:rows, :] + band_ref[0, kind]

        @pl.when(jb == 0)
        def _():
            m_scr[ib % 2] = jnp.full(m_scr.shape[1:], M_INIT, F32)
            l_scr[ib % 2] = jnp.zeros(l_scr.shape[1:], F32)

        @pl.when(jc == 0)
        def _():
            acc_scr[...] = jnp.zeros(acc_scr.shape, F32)

        k_tile = k_ref[key_rows(ja), :]
        v_tile = v_ref[key_rows(jc), :]
        par = ib % 2
        for r0 in range(0, rows, ROW_CHUNK):
            rc = slice(r0, r0 + ROW_CHUNK)
            acc_scr[rc, :] = a_c[rc, :] * acc_scr[rc, :] + jnp.dot(p_c[rc, :], v_tile,
                                                                   preferred_element_type=F32)
            s_a[rc, :] = lax.dot_general(qs_scr[ia % 2, rc, :], k_tile, (((1,), (1,)), ((), ())),
                                         preferred_element_type=F32)
            s = s_b[rc, :]
            m_prev = m_scr[par, rc, :]
            m_new = jnp.maximum(m_prev, jnp.max(s, axis=-1, keepdims=True))
            alpha = jnp.exp2(m_prev - m_new)
            p = jnp.exp2(s - jnp.concatenate([m_new] * (tk // HEAD_W), axis=1))
            l_scr[par, rc, :] = alpha * l_scr[par, rc, :] + jnp.sum(p, axis=-1, keepdims=True)
            m_scr[par, rc, :] = m_new
            p_b[rc, :] = p.astype(BF16)
            a_b[rc, :] = alpha

        @pl.when((jc == last_key_tile(ic)) & (ic < nq))
        def _():
            o = acc_scr[...] / l_scr[ic % 2]
            z = _head_out(o[0:tq], o[tq:rows], _lambda_full(lam_ref, lam_init), subln_ref[...],
                          lam_init)
            o_ref[pl.ds(pl.multiple_of(ic * tq, tq), tq), :] = z.astype(o_ref.dtype)

    def advance(pair):
        i, j = pair
        wrap = j >= last_key_tile(i)
        return jnp.where(wrap, i + 1, i), jnp.where(wrap, 0, j + 1)

    def body(_, carry):
        pair_a, pair_b, pair_c = carry
        half_step(pair_a, pair_b, pair_c, s0, s1, p1, a1, p0, a0)
        pair_a, pair_b, pair_c = advance(pair_a), pair_a, pair_b
        half_step(pair_a, pair_b, pair_c, s1, s0, p0, a0, p1, a1)
        return advance(pair_a), pair_a, pair_b

    idle = (jnp.int32(0), jnp.int32(1))
    lax.fori_loop(0, (n_pairs + 2) // 2, body, ((jnp.int32(0), jnp.int32(0)), idle, idle))


def _attn_prompt(q, kb, vb, band, lam_p, subln, lam_init, batch, seq):
    t = q.shape[0]
    tq, tk = Q_TILE, K_TILE
    rows = 2 * tq
    seq_spec = pl.BlockSpec((seq, HEAD_W), lambda b, h: (b, h))
    return pl.pallas_call(
        functools.partial(_attn_prompt_kernel, lam_init=lam_init),
        grid=(batch, N_HEADS),
        in_specs=[pl.BlockSpec((4, QK_DIM), lambda b, h: (0, 0)),
                  pl.BlockSpec((1, V_DIM), lambda b, h: (0, 0)),
                  seq_spec, seq_spec, seq_spec,
                  pl.BlockSpec((1,) + band.shape[1:], lambda b, h: (h, 0, 0, 0))],
        out_specs=seq_spec,
        out_shape=jax.ShapeDtypeStruct((t, ATTN_W), BF16),
        scratch_shapes=[pltpu.VMEM((2, rows, HEAD_W), BF16),
                        pltpu.VMEM((rows, tk), F32), pltpu.VMEM((rows, tk), F32),
                        pltpu.VMEM((rows, tk), BF16), pltpu.VMEM((rows, tk), BF16),
                        pltpu.VMEM((rows, V_DIM), F32), pltpu.VMEM((rows, V_DIM), F32),
                        pltpu.VMEM((2, rows, V_DIM), F32), pltpu.VMEM((2, rows, V_DIM), F32),
                        pltpu.VMEM((rows, V_DIM), F32)],
        compiler_params=_params(2),
        name="attn_prompt",
    )(lam_p, subln, q, kb, vb, band)


def _page_copies(pt_ref, ck_ref, cv_ref, kbuf, vbuf, sem, seq, slot, layer, n_pages):
    rows = ck_ref.shape[2]
    copies = []
    for p in range(n_pages):
        phys = 0 if seq is None else pt_ref[seq * n_pages + p]
        dst = pl.ds(p * rows, rows)
        copies.append(pltpu.make_async_copy(ck_ref.at[layer, phys], kbuf.at[slot, dst], sem.at[slot, 0]))
        copies.append(pltpu.make_async_copy(cv_ref.at[layer, phys], vbuf.at[slot, dst], sem.at[slot, 1]))
    return copies


def _attn_sample_kernel(pt_ref, lam_ref, subln_ref, q_ref, kn_ref, vn_ref, pastb_ref, newb_ref,
                        ck_ref, cv_ref, o_ref, kbuf, vbuf, kn_scr, vn_scr, sem,
                        *, lam_init, layer, n_pages):
    seq = pl.program_id(0)
    slot = seq % 2
    copies = functools.partial(_page_copies, pt_ref, ck_ref, cv_ref, kbuf, vbuf, sem,
                               layer=layer, n_pages=n_pages)

    @pl.when(seq == 0)
    def _():
        for c in copies(seq=0, slot=0):
            c.start()

    @pl.when(seq + 1 < pl.num_programs(0))
    def _():
        for c in copies(seq=seq + 1, slot=1 - slot):
            c.start()

    dec_seq = q_ref.shape[1]
    q = q_ref[0]
    lane = lax.broadcasted_iota(jnp.int32, (dec_seq, HEAD_W), 1)
    blocks = []
    for h in range(N_HEADS):
        qh = q[:, h * HEAD_W:(h + 1) * HEAD_W]
        blocks += [jnp.where(lane < QK_DIM, qh, 0.0), jnp.where(lane >= QK_DIM, qh, 0.0)]
    qall = jnp.concatenate(blocks, axis=0).astype(BF16)

    def logits(k):
        return lax.dot_general(qall, k, (((1,), (1,)), ((), ())), preferred_element_type=F32)

    n_new = kn_ref.shape[1]
    kn_scr[...] = jnp.zeros(kn_scr.shape, F32)
    vn_scr[...] = jnp.zeros(vn_scr.shape, F32)
    kn_scr[0:n_new, :] = kn_ref[0]
    vn_scr[0:n_new, :] = vn_ref[0]
    s_new = logits(kn_scr[...].astype(BF16)) + newb_ref[...]

    for c in copies(seq=None, slot=slot):
        c.wait()
    s_past = logits(kbuf[slot].astype(BF16)) + pastb_ref[...]
    m = jnp.maximum(jnp.max(s_past, axis=-1, keepdims=True), jnp.max(s_new, axis=-1, keepdims=True))
    p_past = jnp.exp2(s_past - m)
    p_new = jnp.exp2(s_new - m)
    l = jnp.sum(p_past, axis=-1, keepdims=True) + jnp.sum(p_new, axis=-1, keepdims=True)
    acc = (jnp.dot(p_past.astype(BF16), vbuf[slot].astype(BF16), preferred_element_type=F32)
           + jnp.dot(p_new.astype(BF16), vn_scr[...].astype(BF16), preferred_element_type=F32))
    o = acc / l
    lam = _lambda_full(lam_ref, lam_init)
    heads = []
    for h in range(N_HEADS):
        oh = o[2 * dec_seq * h:2 * dec_seq * (h + 1)]
        heads.append(_head_out(oh[0:dec_seq], oh[dec_seq:2 * dec_seq], lam, subln_ref[...], lam_init))
    o_ref[0] = jnp.concatenate(heads, axis=-1)


def _attn_sample(q, k_new, v_new, cache_k, cache_v, layer, page_table, pastb, newb, lam_p, subln,
                 lam_init):
    n_seq, dec_seq, _ = q.shape
    n_pages = page_table.shape[1]
    page_rows = cache_k.shape[2]
    rows = N_HEADS * 2 * dec_seq
    past_rows = n_pages * page_rows
    per_seq = lambda a: pl.BlockSpec((1,) + a.shape[1:], lambda s, pt: (s, 0, 0))
    new_rows = pl.BlockSpec((1,) + k_new.shape[1:], lambda s, pt: (layer * n_seq + s, 0, 0))
    const = lambda shape: pl.BlockSpec(shape, lambda s, pt: (0,) * len(shape),
                                       pipeline_mode=pl.Buffered(1))
    hbm = pl.BlockSpec(memory_space=pl.ANY)
    grid_spec = pltpu.PrefetchScalarGridSpec(
        num_scalar_prefetch=1,
        grid=(n_seq,),
        in_specs=[const((4, QK_DIM)), const((1, V_DIM)), per_seq(q), new_rows, new_rows,
                  const((rows, past_rows)), const((rows, HEAD_W)), hbm, hbm],
        out_specs=per_seq(q),
        scratch_shapes=[pltpu.VMEM((2, past_rows, HEAD_W), F32), pltpu.VMEM((2, past_rows, HEAD_W), F32),
                        pltpu.VMEM((HEAD_W, HEAD_W), F32), pltpu.VMEM((HEAD_W, HEAD_W), F32),
                        pltpu.SemaphoreType.DMA((2, 2))],
    )
    return pl.pallas_call(
        functools.partial(_attn_sample_kernel, lam_init=lam_init, layer=layer, n_pages=n_pages),
        grid_spec=grid_spec,
        out_shape=jax.ShapeDtypeStruct((n_seq, dec_seq, ATTN_W), F32),
        compiler_params=_params(1),
        name="attn_sample",
    )(page_table.reshape(-1), lam_p, subln, q, k_new, v_new, pastb, newb, cache_k, cache_v)


def _sample_bias_tiles(table, dec_seq, page, n_pages):
    rows = N_HEADS * 2 * dec_seq
    lastb = _bias_tiles(table, 2 * dec_seq, page, dec_seq, page, page).reshape(rows, page)
    newb = _bias_tiles(table, 2 * dec_seq, page, dec_seq, 0, dec_seq).reshape(rows, page)
    row_head = jnp.arange(rows)[:, None] // (2 * dec_seq)
    own = lambda n_tok: (jnp.arange(n_tok * N_HEADS)[None, :] % N_HEADS) == row_head
    spread = lambda a: jnp.repeat(a, N_HEADS, axis=1)
    generic = jnp.where(own(page), 0.0, MASKED).astype(F32)
    last = jnp.where(own(page), spread(lastb), MASKED)
    pastb = jnp.concatenate([jnp.tile(generic, (1, n_pages - 1)), last], axis=1)
    new = jnp.where(own(dec_seq), spread(newb[:, :dec_seq]), MASKED)
    newb_full = jnp.concatenate(
        [new, jnp.full((rows, HEAD_W - dec_seq * N_HEADS), MASKED, F32)], axis=1)
    return pastb, newb_full


def _proj_out_ln_kernel(x_ref, zc_ref, za_ref, wc_ref, wa_ref, g_ref, b_ref, o_ref):
    mix = (jnp.dot(zc_ref[...].astype(BF16), wc_ref[...], preferred_element_type=F32)
           + jnp.dot(za_ref[...].astype(BF16), wa_ref[...], preferred_element_type=F32))
    o_ref[...] = _layer_norm(ALPHA * x_ref[...] + mix, g_ref[...], b_ref[...])


def _proj_out_ln(x, zc, za, w_out, layer, g, b):
    t = x.shape[0]
    tm = min(TOKEN_TILE, t)
    row = lambda w: pl.BlockSpec((tm, w), lambda i: (i, 0))
    return pl.pallas_call(
        _proj_out_ln_kernel,
        grid=(t // tm,),
        in_specs=[row(D_MODEL), row(CONV_CH), row(ATTN_W), _layer_resident(w_out, layer, CONV_CH, 0),
                  _layer_resident(w_out, layer, ATTN_W, CONV_CH // ATTN_W),
                  _resident((1, D_MODEL)), _resident((1, D_MODEL))],
        out_specs=row(D_MODEL),
        out_shape=jax.ShapeDtypeStruct((t, D_MODEL), F32),
        compiler_params=_params(1),
        name="proj_out_ln",
    )(x, zc, za, w_out, w_out, g, b)


def kernel(x_prompt, x_sample, cache_k, cache_v, state_conv, page_table, rel_bias_table,
           w_in, w_out, conv_w, lambda_q1, lambda_k1, lambda_q2, lambda_k2, subln_w,
           ln_g, ln_b, ffn1_w_gate, ffn1_w_up, ffn1_w_down,
           ffn2_w_gate, ffn2_w_up, ffn2_w_down):
    batch, seq, _ = x_prompt.shape
    n_dec, dec_seq, _ = x_sample.shape
    depth, n_phys, page = cache_k.shape[:3]

    xp = x_prompt.reshape(batch * seq, D_MODEL)
    xs = x_sample.reshape(n_dec * dec_seq, D_MODEL)
    ck = cache_k.reshape(depth, n_phys, page * N_HEADS, HEAD_W)
    cv = cache_v.reshape(depth, n_phys, page * N_HEADS, HEAD_W)
    bf = lambda w: w.astype(BF16)
    f1 = (bf(ffn1_w_gate), bf(ffn1_w_up), bf(ffn1_w_down))
    f2 = (bf(ffn2_w_gate), bf(ffn2_w_up), bf(ffn2_w_down))
    wi, wo = bf(w_in), bf(w_out)

    band = jnp.stack([_bias_tiles(rel_bias_table, Q_TILE, K_TILE, Q_TILE, n * Q_TILE, K_TILE)
                      for n in range(K_TILE // Q_TILE + 1)], axis=1)
    pastb, newb = _sample_bias_tiles(rel_bias_table, dec_seq, page, page_table.shape[1])

    head_rows = lambda tokens: jnp.zeros((depth * tokens * N_HEADS, HEAD_W), F32)
    kp, vp = head_rows(batch * seq), head_rows(batch * seq)
    ks, vs = head_rows(n_dec * dec_seq), head_rows(n_dec * dec_seq)
    conv_p, conv_s = [], []
    for l in range(depth):
        lam_init = 0.8 - 0.6 * math.exp(-0.3 * l)
        lam_p = jnp.stack([lambda_q1[l], lambda_k1[l], lambda_q2[l], lambda_k2[l]])
        subln = subln_w[l].reshape(1, V_DIM)
        g = lambda j: ln_g[l, j].reshape(1, D_MODEL)
        b = lambda j: ln_b[l, j].reshape(1, D_MODEL)

        xp = _ffn_ln(xp, *f1, l, g(0), b(0))
        zc, q, kp, vp, kb, vb, cst = _proj_in_prompt(xp, wi, conv_w, kp, vp, l, batch, seq)
        za = _attn_prompt(q, kb, vb, band, lam_p, subln, lam_init, batch, seq)
        xp = _proj_out_ln(xp, zc, za, wo, l, g(1), b(1))
        xp = _ffn_ln(xp, *f2, l, g(2), b(2))
        conv_p.append(cst)

        xs = _ffn_ln(xs, *f1, l, g(0), b(0))
        zc, q, ks, vs, cst = _proj_in_sample(xs, wi, conv_w, ks, vs, l, state_conv[l], dec_seq)
        per_seq = lambda a: a.reshape(depth * n_dec, dec_seq * N_HEADS, HEAD_W)
        za = _attn_sample(q.reshape(n_dec, dec_seq, ATTN_W), per_seq(ks), per_seq(vs), ck, cv, l,
                          page_table, pastb, newb, lam_p, subln, lam_init)
        xs = _proj_out_ln(xs, zc, za.reshape(n_dec * dec_seq, ATTN_W), wo, l, g(1), b(1))
        xs = _ffn_ln(xs, *f2, l, g(2), b(2))
        conv_s.append(cst)

    return (xp.reshape(batch, seq, D_MODEL), xs.reshape(n_dec, dec_seq, D_MODEL),
            kp.reshape(depth, batch, seq, N_HEADS, HEAD_W), vp.reshape(depth, batch, seq, N_HEADS, V_DIM),
            jnp.stack(conv_p),
            ks.reshape(depth, n_dec, dec_seq, N_HEADS, HEAD_W),
            vs.reshape(depth, n_dec, dec_seq, N_HEADS, V_DIM), jnp.stack(conv_s))
```

```python
import functools
import math

import jax
import jax.numpy as jnp
from jax import lax
from jax.experimental import pallas as pl
from jax.experimental.pallas import tpu as pltpu

F32 = jnp.float32
BF16 = jnp.bfloat16

D_MODEL = 1024
DEPTH = 2
CONV_CH = D_MODEL // 2
CONV_K = 3
N_HEADS = 4
QK_DIM = 64
V_DIM = 2 * QK_DIM
HEAD_W = 2 * QK_DIM
ATTN_W = N_HEADS * V_DIM
D_FF = 2816
NUM_BUCKETS = 32
MAX_DISTANCE = 128
LN_EPS = 1e-5
ATTN_SCALE = QK_DIM ** -0.5
LOG2E = math.log2(math.e)
LOGIT_SCALE = ATTN_SCALE * LOG2E
ALPHA = (2 * DEPTH) ** 0.25
MASKED = -1e30

VMEM_LIMIT_BYTES = 56 * 1024 * 1024
TOKEN_TILE = 512
FF_CHUNK = 256
Q_TILE = 512
K_TILE = 512
SUBLANES = 8


def _params(n_axes):
    return pltpu.CompilerParams(dimension_semantics=("arbitrary",) * n_axes,
                                vmem_limit_bytes=VMEM_LIMIT_BYTES)


def _resident(shape):
    return pl.BlockSpec(shape, lambda *_: (0,) * len(shape), pipeline_mode=pl.Buffered(1))


def _layer_resident(stacked, layer, block_rows=None, row_block=0):
    _, rows, cols = stacked.shape
    return pl.BlockSpec((None, block_rows or rows, cols), lambda *_: (layer, row_block, 0),
                        pipeline_mode=pl.Buffered(1))


def _layer_norm(x, g, b):
    mu = jnp.mean(x, axis=-1, keepdims=True)
    xc = x - mu
    var = jnp.mean(xc * xc, axis=-1, keepdims=True)
    return xc * lax.rsqrt(var + LN_EPS) * g + b


def _swiglu_ln(x, wg_ref, wu_ref, wd_ref, g_ref, b_ref, h_scr):
    xb = x.astype(BF16)
    for c in range(D_FF // FF_CHUNK):
        sl = slice(c * FF_CHUNK, (c + 1) * FF_CHUNK)
        g = jnp.dot(xb, wg_ref[:, sl], preferred_element_type=F32)
        u = jnp.dot(xb, wu_ref[:, sl], preferred_element_type=F32)
        h_scr[:, sl] = (g * jax.nn.sigmoid(g) * u).astype(BF16)
    y = jnp.dot(h_scr[...], wd_ref[...], preferred_element_type=F32)
    return _layer_norm(ALPHA * x + 0.5 * y, g_ref[...], b_ref[...])


def _ffn_ln_kernel(x_ref, wg_ref, wu_ref, wd_ref, g_ref, b_ref, o_ref, h_scr):
    o_ref[...] = _swiglu_ln(x_ref[...], wg_ref, wu_ref, wd_ref, g_ref, b_ref, h_scr)


def _ffn_ln(x, wg, wu, wd, layer, g, b):
    t = x.shape[0]
    tm = min(TOKEN_TILE, t)
    row = pl.BlockSpec((tm, D_MODEL), lambda i: (i, 0))
    return pl.pallas_call(
        _ffn_ln_kernel,
        grid=(t // tm,),
        in_specs=[row, _layer_resident(wg, layer), _layer_resident(wu, layer),
                  _layer_resident(wd, layer), _resident((1, D_MODEL)), _resident((1, D_MODEL))],
        out_specs=row,
        out_shape=jax.ShapeDtypeStruct((t, D_MODEL), F32),
        scratch_shapes=[pltpu.VMEM((tm, D_FF), BF16)],
        compiler_params=_params(1),
        name="ffn_ln",
    )(x, wg, wu, wd, g, b)


def _section(xb, w_ref, c):
    return jnp.dot(xb, w_ref[:, c * CONV_CH:(c + 1) * CONV_CH], preferred_element_type=F32)


def _store_head_rows(ref, x):
    tokens = x.shape[0]
    for h in range(N_HEADS):
        ref[pl.ds(h, tokens, stride=N_HEADS), :] = x[:, h * HEAD_W:(h + 1) * HEAD_W]


def _emit_qkv(xb, w_ref, q_ref, k_ref, v_ref, kb_ref, vb_ref):
    q_ref[...] = (_section(xb, w_ref, 3) * LOGIT_SCALE).astype(q_ref.dtype)
    k = _section(xb, w_ref, 4)
    _store_head_rows(k_ref, k)
    v = _section(xb, w_ref, 5)
    _store_head_rows(v_ref, v)
    if kb_ref is not None:
        kb_ref[...] = k.astype(BF16)
        vb_ref[...] = v.astype(BF16)


def _proj_in_prompt_kernel(x_ref, w_ref, cw_ref, k_all_ref, v_all_ref, zc_ref, q_ref, k_ref, v_ref,
                           kb_ref, vb_ref, cs_ref, u_scr, *, tiles_per_seq):
    del k_all_ref, v_all_ref
    tm = x_ref.shape[0]
    xb = x_ref[...].astype(BF16)
    bg = _section(xb, w_ref, 0)
    u = _section(xb, w_ref, 1) * _section(xb, w_ref, 2)

    @pl.when(pl.program_id(0) % tiles_per_seq == 0)
    def _():
        u_scr[0:SUBLANES, :] = jnp.zeros((SUBLANES, CONV_CH), F32)

    u_scr[SUBLANES:SUBLANES + tm, :] = u
    um1 = u_scr[SUBLANES - 1:SUBLANES - 1 + tm, :]
    um2 = u_scr[SUBLANES - 2:SUBLANES - 2 + tm, :]
    cw = cw_ref[...]
    y = cw[0:1] * um2 + cw[1:2] * um1 + cw[2:3] * u
    zc_ref[...] = (bg * y).astype(zc_ref.dtype)
    u_scr[0:SUBLANES, :] = u_scr[tm:tm + SUBLANES, :]
    cs_ref[0] = u[tm - (CONV_K - 1):tm, :]
    _emit_qkv(xb, w_ref, q_ref, k_ref, v_ref, kb_ref, vb_ref)


def _head_rows_spec(tm, n_tiles, layer):
    return pl.BlockSpec((tm * N_HEADS, HEAD_W), lambda i: (layer * n_tiles + i, 0))


def _proj_in_prompt(x, w_in, conv_w, k_all, v_all, layer, batch, seq):
    t = x.shape[0]
    tm = TOKEN_TILE
    tiles_per_seq = seq // tm
    row = lambda w: pl.BlockSpec((tm, w), lambda i: (i, 0))
    hbm = pl.BlockSpec(memory_space=pl.ANY)
    head_rows = _head_rows_spec(tm, t // tm, layer)
    same = lambda a: jax.ShapeDtypeStruct(a.shape, a.dtype)
    return pl.pallas_call(
        functools.partial(_proj_in_prompt_kernel, tiles_per_seq=tiles_per_seq),
        grid=(t // tm,),
        in_specs=[row(D_MODEL), _layer_resident(w_in, layer), _layer_resident(conv_w, layer), hbm, hbm],
        out_specs=[row(CONV_CH), row(ATTN_W), head_rows, head_rows, row(ATTN_W), row(ATTN_W),
                   pl.BlockSpec((1, CONV_K - 1, CONV_CH), lambda i: (i // tiles_per_seq, 0, 0))],
        out_shape=[jax.ShapeDtypeStruct((t, CONV_CH), BF16),
                   jax.ShapeDtypeStruct((t, ATTN_W), BF16),
                   same(k_all), same(v_all),
                   jax.ShapeDtypeStruct((t, ATTN_W), BF16),
                   jax.ShapeDtypeStruct((t, ATTN_W), BF16),
                   jax.ShapeDtypeStruct((batch, CONV_K - 1, CONV_CH), F32)],
        input_output_aliases={3: 2, 4: 3},
        scratch_shapes=[pltpu.VMEM((tm + 2 * SUBLANES, CONV_CH), F32)],
        compiler_params=_params(1),
        name="proj_in_prompt",
    )(x, w_in, conv_w, k_all, v_all)


def _proj_in_sample_kernel(x_ref, w_ref, cw_ref, p1_ref, p2_ref, k_all_ref, v_all_ref, zc_ref, q_ref,
                           k_ref, v_ref, cs_ref, u_scr, *, dec_seq):
    del k_all_ref, v_all_ref
    tm = x_ref.shape[0]
    xb = x_ref[...].astype(BF16)
    bg = _section(xb, w_ref, 0)
    u = _section(xb, w_ref, 1) * _section(xb, w_ref, 2)
    u_scr[0:SUBLANES, :] = jnp.zeros((SUBLANES, CONV_CH), F32)
    u_scr[SUBLANES:SUBLANES + tm, :] = u
    tpos = lax.broadcasted_iota(jnp.int32, (tm, CONV_CH), 0) % dec_seq
    um1 = jnp.where(tpos >= 1, u_scr[SUBLANES - 1:SUBLANES - 1 + tm, :], p1_ref[...])
    um2 = jnp.where(tpos >= 2, u_scr[SUBLANES - 2:SUBLANES - 2 + tm, :], p2_ref[...])
    cw = cw_ref[...]
    y = cw[0:1] * um2 + cw[1:2] * um1 + cw[2:3] * u
    zc_ref[...] = (bg * y).astype(zc_ref.dtype)
    cs_ref[...] = u.reshape(tm // dec_seq, dec_seq, CONV_CH)[:, dec_seq - (CONV_K - 1):, :]
    _emit_qkv(xb, w_ref, q_ref, k_ref, v_ref, None, None)


def _proj_in_sample(x, w_in, conv_w, k_all, v_all, layer, prev, dec_seq):
    t = x.shape[0]
    n_seq = t // dec_seq
    tm = min(TOKEN_TILE, t)
    p2 = jnp.pad(prev, ((0, 0), (0, dec_seq - (CONV_K - 1)), (0, 0))).reshape(t, CONV_CH)
    p1 = jnp.pad(prev[:, 1:], ((0, 0), (0, dec_seq - 1), (0, 0))).reshape(t, CONV_CH)
    row = lambda w: pl.BlockSpec((tm, w), lambda i: (i, 0))
    hbm = pl.BlockSpec(memory_space=pl.ANY)
    head_rows = _head_rows_spec(tm, t // tm, layer)
    same = lambda a: jax.ShapeDtypeStruct(a.shape, a.dtype)
    return pl.pallas_call(
        functools.partial(_proj_in_sample_kernel, dec_seq=dec_seq),
        grid=(t // tm,),
        in_specs=[row(D_MODEL), _layer_resident(w_in, layer), _layer_resident(conv_w, layer),
                  row(CONV_CH), row(CONV_CH), hbm, hbm],
        out_specs=[row(CONV_CH), row(ATTN_W), head_rows, head_rows,
                   pl.BlockSpec((tm // dec_seq, CONV_K - 1, CONV_CH), lambda i: (i, 0, 0))],
        out_shape=[jax.ShapeDtypeStruct((t, CONV_CH), BF16),
                   jax.ShapeDtypeStruct((t, ATTN_W), F32),
                   same(k_all), same(v_all),
                   jax.ShapeDtypeStruct((n_seq, CONV_K - 1, CONV_CH), F32)],
        input_output_aliases={5: 2, 6: 3},
        scratch_shapes=[pltpu.VMEM((tm + 2 * SUBLANES, CONV_CH), F32)],
        compiler_params=_params(1),
        name="proj_in_sample",
    )(x, w_in, conv_w, p1, p2, k_all, v_all)


def _bias_kernel(tab_ref, o_ref, *, rmod, offset, cmax):
    h = pl.program_id(0)
    shape = o_ref.shape[1:]
    r = lax.broadcasted_iota(jnp.int32, shape, 0)
    c = lax.broadcasted_iota(jnp.int32, shape, 1)
    dist = (r % rmod) - c + offset
    n = jnp.maximum(dist, 0)
    max_exact = NUM_BUCKETS // 2
    nf = jnp.maximum(n, 1).astype(F32)
    large = max_exact + (jnp.log(nf / max_exact) / math.log(MAX_DISTANCE / max_exact)
                         * (NUM_BUCKETS - max_exact)).astype(jnp.int32)
    large = jnp.minimum(large, NUM_BUCKETS - 1)
    bucket = jnp.where(n < max_exact, n, large)
    last = tab_ref[NUM_BUCKETS - 1, h]
    bias = jnp.zeros(shape, F32)
    for b in range(NUM_BUCKETS - 1):
        bias = jnp.where(bucket == b, (tab_ref[b, h] - last) * LOG2E, bias)
    o_ref[0] = jnp.where((dist >= 0) & (c < cmax), bias, MASKED)


def _bias_tiles(table, rows, cols, rmod, offset, cmax):
    return pl.pallas_call(
        functools.partial(_bias_kernel, rmod=rmod, offset=offset, cmax=cmax),
        grid=(N_HEADS,),
        in_specs=[pl.BlockSpec(memory_space=pltpu.SMEM)],
        out_specs=pl.BlockSpec((1, rows, cols), lambda h: (h, 0, 0)),
        out_shape=jax.ShapeDtypeStruct((N_HEADS, rows, cols), F32),
        compiler_params=_params(1),
        name="rel_bias_tiles",
    )(table)


def _lambda_full(lam_ref, lam_init):
    lp = lam_ref[...]
    d1 = jnp.sum(lp[0:1] * lp[1:2], axis=-1, keepdims=True)
    d2 = jnp.sum(lp[2:3] * lp[3:4], axis=-1, keepdims=True)
    return jnp.exp(d1) - jnp.exp(d2) + lam_init


def _head_out(o1, o2, lam, subln, lam_init):
    od = o1 - lam * o2
    ms = jnp.mean(od * od, axis=-1, keepdims=True)
    return od * lax.rsqrt(ms + LN_EPS) * subln * (1.0 - lam_init)


M_INIT = -1e29


def _attn_prompt_kernel(lam_ref, subln_ref, q_ref, k_ref, v_ref, band_ref, o_ref,
                        qs_scr, s0, s1, p0, p1, a0, a1, m_scr, l_scr, acc_scr, *, lam_init):
    tq, tk = Q_TILE, K_TILE
    ratio = tk // tq
    rows = 2 * tq
    nq = q_ref.shape[0] // tq
    last_key_tile = lambda i: i // ratio
    n_pairs = sum(last_key_tile(i) + 1 for i in range(nq))
    assert (n_pairs + 2) % 2 == 0
    lane = lax.broadcasted_iota(jnp.int32, (tq, HEAD_W), 1)

    s0[...] = jnp.full(s0.shape, MASKED, F32)
    s1[...] = jnp.full(s1.shape, MASKED, F32)
    p0[...] = jnp.zeros(p0.shape, BF16)
    p1[...] = jnp.zeros(p1.shape, BF16)
    a0[...] = jnp.ones(a0.shape, F32)
    a1[...] = jnp.ones(a1.shape, F32)
    m_scr[...] = jnp.full(m_scr.shape, M_INIT, F32)
    l_scr[...] = jnp.zeros(l_scr.shape, F32)
    acc_scr[...] = jnp.zeros(acc_scr.shape, F32)
    qs_scr[...] = jnp.zeros(qs_scr.shape, BF16)

    def key_rows(j):
        return pl.ds(pl.multiple_of(j * tk, tk), tk)

    def half_step(pair_a, pair_b, pair_c, s_a, s_b, p_b, a_b, p_c, a_c):
        ia, ja = pair_a
        ib, jb = pair_b
        ic, jc = pair_c

        @pl.when((ja == 0) & (ia < nq))
        def _():
            q = q_ref[pl.ds(pl.multiple_of(ia * tq, tq), tq), :]
            zero = jnp.zeros_like(q)
            qs_scr[ia % 2, 0:tq, :] = jnp.where(lane < QK_DIM, q, zero)
            qs_scr[ia % 2, tq:rows, :] = jnp.where(lane >= QK_DIM, q, zero)

        @pl.when(ib >= nq)
        def _():
            s_b[...] = jnp.full(s_b.shape, MASKED, F32)

        for kind in range(ratio + 1):
            @pl.when((ib < nq) & (ib - jb * ratio == kind))
            def _():
                s_b[0:tq, :] = s_b[0:tq, :] + band_ref[0, kind]
                s_b[tq:rows, :] = s_b[tq:rows, :] + band_ref[0, kind]

        @pl.when(jb == 0)
        def _():
            m_scr[ib % 2] = jnp.full(m_scr.shape[1:], M_INIT, F32)
            l_scr[ib % 2] = jnp.zeros(l_scr.shape[1:], F32)

        @pl.when(jc == 0)
        def _():
            acc_scr[...] = jnp.zeros(acc_scr.shape, F32)

        acc_scr[...] = a_c[...] * acc_scr[...] + jnp.dot(p_c[...], v_ref[key_rows(jc), :],
                                                         preferred_element_type=F32)
        s_a[...] = lax.dot_general(qs_scr[ia % 2], k_ref[key_rows(ja), :], (((1,), (1,)), ((), ())),
                                   preferred_element_type=F32)
        par = ib % 2
        s = s_b[...]
        m_prev = m_scr[par]
        m_new = jnp.maximum(m_prev, jnp.max(s, axis=-1, keepdims=True))
        alpha = jnp.exp2(m_prev - m_new)
        p = jnp.exp2(s - jnp.concatenate([m_new] * (tk // HEAD_W), axis=1))
        l_scr[par] = alpha * l_scr[par] + jnp.sum(p, axis=-1, keepdims=True)
        m_scr[par] = m_new
        p_b[...] = p.astype(BF16)
        a_b[...] = alpha

        @pl.when((jc == last_key_tile(ic)) & (ic < nq))
        def _():
            o = acc_scr[...] / l_scr[ic % 2]
            z = _head_out(o[0:tq], o[tq:rows], _lambda_full(lam_ref, lam_init), subln_ref[...],
                          lam_init)
            o_ref[pl.ds(pl.multiple_of(ic * tq, tq), tq), :] = z.astype(o_ref.dtype)

    def advance(pair):
        i, j = pair
        wrap = j >= last_key_tile(i)
        return jnp.where(wrap, i + 1, i), jnp.where(wrap, 0, j + 1)

    def body(_, carry):
        pair_a, pair_b, pair_c = carry
        half_step(pair_a, pair_b, pair_c, s0, s1, p1, a1, p0, a0)
        pair_a, pair_b, pair_c = advance(pair_a), pair_a, pair_b
        half_step(pair_a, pair_b, pair_c, s1, s0, p0, a0, p1, a1)
        return advance(pair_a), pair_a, pair_b

    idle = (jnp.int32(0), jnp.int32(1))
    lax.fori_loop(0, (n_pairs + 2) // 2, body, ((jnp.int32(0), jnp.int32(0)), idle, idle))


def _attn_prompt(q, kb, vb, band, lam_p, subln, lam_init, batch, seq):
    t = q.shape[0]
    tq, tk = Q_TILE, K_TILE
    rows = 2 * tq
    seq_spec = pl.BlockSpec((seq, HEAD_W), lambda b, h: (b, h))
    return pl.pallas_call(
        functools.partial(_attn_prompt_kernel, lam_init=lam_init),
        grid=(batch, N_HEADS),
        in_specs=[pl.BlockSpec((4, QK_DIM), lambda b, h: (0, 0)),
                  pl.BlockSpec((1, V_DIM), lambda b, h: (0, 0)),
                  seq_spec, seq_spec, seq_spec,
                  pl.BlockSpec((1,) + band.shape[1:], lambda b, h: (h, 0, 0, 0))],
        out_specs=seq_spec,
        out_shape=jax.ShapeDtypeStruct((t, ATTN_W), BF16),
        scratch_shapes=[pltpu.VMEM((2, rows, HEAD_W), BF16),
                        pltpu.VMEM((rows, tk), F32), pltpu.VMEM((rows, tk), F32),
                        pltpu.VMEM((rows, tk), BF16), pltpu.VMEM((rows, tk), BF16),
                        pltpu.VMEM((rows, V_DIM), F32), pltpu.VMEM((rows, V_DIM), F32),
                        pltpu.VMEM((2, rows, V_DIM), F32), pltpu.VMEM((2, rows, V_DIM), F32),
                        pltpu.VMEM((rows, V_DIM), F32)],
        compiler_params=_params(2),
        name="attn_prompt",
    )(lam_p, subln, q, kb, vb, band)


def _page_copies(pt_ref, ck_ref, cv_ref, kbuf, vbuf, sem, seq, slot, layer, n_pages):
    rows = ck_ref.shape[2]
    copies = []
    for p in range(n_pages):
        phys = 0 if seq is None else pt_ref[seq * n_pages + p]
        dst = pl.ds(p * rows, rows)
        copies.append(pltpu.make_async_copy(ck_ref.at[layer, phys], kbuf.at[slot, dst], sem.at[slot, 0]))
        copies.append(pltpu.make_async_copy(cv_ref.at[layer, phys], vbuf.at[slot, dst], sem.at[slot, 1]))
    return copies


def _attn_sample_kernel(pt_ref, lam_ref, subln_ref, q_ref, kn_ref, vn_ref, pastb_ref, newb_ref,
                        ck_ref, cv_ref, o_ref, kbuf, vbuf, kn_scr, vn_scr, sem,
                        *, lam_init, layer, n_pages):
    seq = pl.program_id(0)
    slot = seq % 2
    copies = functools.partial(_page_copies, pt_ref, ck_ref, cv_ref, kbuf, vbuf, sem,
                               layer=layer, n_pages=n_pages)

    @pl.when(seq == 0)
    def _():
        for c in copies(seq=0, slot=0):
            c.start()

    @pl.when(seq + 1 < pl.num_programs(0))
    def _():
        for c in copies(seq=seq + 1, slot=1 - slot):
            c.start()

    dec_seq = q_ref.shape[1]
    q = q_ref[0]
    lane = lax.broadcasted_iota(jnp.int32, (dec_seq, HEAD_W), 1)
    blocks = []
    for h in range(N_HEADS):
        qh = q[:, h * HEAD_W:(h + 1) * HEAD_W]
        blocks += [jnp.where(lane < QK_DIM, qh, 0.0), jnp.where(lane >= QK_DIM, qh, 0.0)]
    qall = jnp.concatenate(blocks, axis=0).astype(BF16)

    def logits(k):
        return lax.dot_general(qall, k, (((1,), (1,)), ((), ())), preferred_element_type=F32)

    n_new = kn_ref.shape[1]
    kn_scr[...] = jnp.zeros(kn_scr.shape, F32)
    vn_scr[...] = jnp.zeros(vn_scr.shape, F32)
    kn_scr[0:n_new, :] = kn_ref[0]
    vn_scr[0:n_new, :] = vn_ref[0]
    s_new = logits(kn_scr[...].astype(BF16)) + newb_ref[...]

    for c in copies(seq=None, slot=slot):
        c.wait()
    s_past = logits(kbuf[slot].astype(BF16)) + pastb_ref[...]
    m = jnp.maximum(jnp.max(s_past, axis=-1, keepdims=True), jnp.max(s_new, axis=-1, keepdims=True))
    p_past = jnp.exp2(s_past - m)
    p_new = jnp.exp2(s_new - m)
    l = jnp.sum(p_past, axis=-1, keepdims=True) + jnp.sum(p_new, axis=-1, keepdims=True)
    acc = (jnp.dot(p_past.astype(BF16), vbuf[slot].astype(BF16), preferred_element_type=F32)
           + jnp.dot(p_new.astype(BF16), vn_scr[...].astype(BF16), preferred_element_type=F32))
    o = acc / l
    lam = _lambda_full(lam_ref, lam_init)
    heads = []
    for h in range(N_HEADS):
        oh = o[2 * dec_seq * h:2 * dec_seq * (h + 1)]
        heads.append(_head_out(oh[0:dec_seq], oh[dec_seq:2 * dec_seq], lam, subln_ref[...], lam_init))
    o_ref[0] = jnp.concatenate(heads, axis=-1)


def _attn_sample(q, k_new, v_new, cache_k, cache_v, layer, page_table, pastb, newb, lam_p, subln,
                 lam_init):
    n_seq, dec_seq, _ = q.shape
    n_pages = page_table.shape[1]
    page_rows = cache_k.shape[2]
    rows = N_HEADS * 2 * dec_seq
    past_rows = n_pages * page_rows
    per_seq = lambda a: pl.BlockSpec((1,) + a.shape[1:], lambda s, pt: (s, 0, 0))
    new_rows = pl.BlockSpec((1,) + k_new.shape[1:], lambda s, pt: (layer * n_seq + s, 0, 0))
    const = lambda shape: pl.BlockSpec(shape, lambda s, pt: (0,) * len(shape),
                                       pipeline_mode=pl.Buffered(1))
    hbm = pl.BlockSpec(memory_space=pl.ANY)
    grid_spec = pltpu.PrefetchScalarGridSpec(
        num_scalar_prefetch=1,
        grid=(n_seq,),
        in_specs=[const((4, QK_DIM)), const((1, V_DIM)), per_seq(q), new_rows, new_rows,
                  const((rows, past_rows)), const((rows, HEAD_W)), hbm, hbm],
        out_specs=per_seq(q),
        scratch_shapes=[pltpu.VMEM((2, past_rows, HEAD_W), F32), pltpu.VMEM((2, past_rows, HEAD_W), F32),
                        pltpu.VMEM((HEAD_W, HEAD_W), F32), pltpu.VMEM((HEAD_W, HEAD_W), F32),
                        pltpu.SemaphoreType.DMA((2, 2))],
    )
    return pl.pallas_call(
        functools.partial(_attn_sample_kernel, lam_init=lam_init, layer=layer, n_pages=n_pages),
        grid_spec=grid_spec,
        out_shape=jax.ShapeDtypeStruct((n_seq, dec_seq, ATTN_W), F32),
        compiler_params=_params(1),
        name="attn_sample",
    )(page_table.reshape(-1), lam_p, subln, q, k_new, v_new, pastb, newb, cache_k, cache_v)


def _sample_bias_tiles(table, dec_seq, page, n_pages):
    rows = N_HEADS * 2 * dec_seq
    lastb = _bias_tiles(table, 2 * dec_seq, page, dec_seq, page, page).reshape(rows, page)
    newb = _bias_tiles(table, 2 * dec_seq, page, dec_seq, 0, dec_seq).reshape(rows, page)
    row_head = jnp.arange(rows)[:, None] // (2 * dec_seq)
    own = lambda n_tok: (jnp.arange(n_tok * N_HEADS)[None, :] % N_HEADS) == row_head
    spread = lambda a: jnp.repeat(a, N_HEADS, axis=1)
    generic = jnp.where(own(page), 0.0, MASKED).astype(F32)
    last = jnp.where(own(page), spread(lastb), MASKED)
    pastb = jnp.concatenate([jnp.tile(generic, (1, n_pages - 1)), last], axis=1)
    new = jnp.where(own(dec_seq), spread(newb[:, :dec_seq]), MASKED)
    newb_full = jnp.concatenate(
        [new, jnp.full((rows, HEAD_W - dec_seq * N_HEADS), MASKED, F32)], axis=1)
    return pastb, newb_full


def _mix_ffn_ln_kernel(x_ref, zc_ref, za_ref, wc_ref, wa_ref, g1_ref, b1_ref,
                       wg_ref, wu_ref, wd_ref, g2_ref, b2_ref, o_ref, h_scr):
    mix = (jnp.dot(zc_ref[...].astype(BF16), wc_ref[...], preferred_element_type=F32)
           + jnp.dot(za_ref[...].astype(BF16), wa_ref[...], preferred_element_type=F32))
    x = _layer_norm(ALPHA * x_ref[...] + mix, g1_ref[...], b1_ref[...])
    o_ref[...] = _swiglu_ln(x, wg_ref, wu_ref, wd_ref, g2_ref, b2_ref, h_scr)


def _mix_ffn_ln(x, zc, za, w_out, wg, wu, wd, layer, g1, b1, g2, b2):
    t = x.shape[0]
    tm = min(TOKEN_TILE, t)
    row = lambda w: pl.BlockSpec((tm, w), lambda i: (i, 0))
    vec = _resident((1, D_MODEL))
    return pl.pallas_call(
        _mix_ffn_ln_kernel,
        grid=(t // tm,),
        in_specs=[row(D_MODEL), row(CONV_CH), row(ATTN_W), _layer_resident(w_out, layer, CONV_CH, 0),
                  _layer_resident(w_out, layer, ATTN_W, CONV_CH // ATTN_W), vec, vec,
                  _layer_resident(wg, layer), _layer_resident(wu, layer), _layer_resident(wd, layer),
                  vec, vec],
        out_specs=row(D_MODEL),
        out_shape=jax.ShapeDtypeStruct((t, D_MODEL), F32),
        scratch_shapes=[pltpu.VMEM((tm, D_FF), BF16)],
        compiler_params=_params(1),
        name="mix_ffn_ln",
    )(x, zc, za, w_out, w_out, g1, b1, wg, wu, wd, g2, b2)


def kernel(x_prompt, x_sample, cache_k, cache_v, state_conv, page_table, rel_bias_table,
           w_in, w_out, conv_w, lambda_q1, lambda_k1, lambda_q2, lambda_k2, subln_w,
           ln_g, ln_b, ffn1_w_gate, ffn1_w_up, ffn1_w_down,
           ffn2_w_gate, ffn2_w_up, ffn2_w_down):
    batch, seq, _ = x_prompt.shape
    n_dec, dec_seq, _ = x_sample.shape
    depth, n_phys, page = cache_k.shape[:3]

    xp = x_prompt.reshape(batch * seq, D_MODEL)
    xs = x_sample.reshape(n_dec * dec_seq, D_MODEL)
    ck = cache_k.reshape(depth, n_phys, page * N_HEADS, HEAD_W)
    cv = cache_v.reshape(depth, n_phys, page * N_HEADS, HEAD_W)
    bf = lambda w: w.astype(BF16)
    f1 = (bf(ffn1_w_gate), bf(ffn1_w_up), bf(ffn1_w_down))
    f2 = (bf(ffn2_w_gate), bf(ffn2_w_up), bf(ffn2_w_down))
    wi, wo = bf(w_in), bf(w_out)

    band = jnp.stack([_bias_tiles(rel_bias_table, Q_TILE, K_TILE, Q_TILE, n * Q_TILE, K_TILE)
                      for n in range(K_TILE // Q_TILE + 1)], axis=1)
    pastb, newb = _sample_bias_tiles(rel_bias_table, dec_seq, page, page_table.shape[1])

    head_rows = lambda tokens: jnp.zeros((depth * tokens * N_HEADS, HEAD_W), F32)
    kp, vp = head_rows(batch * seq), head_rows(batch * seq)
    ks, vs = head_rows(n_dec * dec_seq), head_rows(n_dec * dec_seq)
    conv_p, conv_s = [], []
    for l in range(depth):
        lam_init = 0.8 - 0.6 * math.exp(-0.3 * l)
        lam_p = jnp.stack([lambda_q1[l], lambda_k1[l], lambda_q2[l], lambda_k2[l]])
        subln = subln_w[l].reshape(1, V_DIM)
        g = lambda j: ln_g[l, j].reshape(1, D_MODEL)
        b = lambda j: ln_b[l, j].reshape(1, D_MODEL)

        xp = _ffn_ln(xp, *f1, l, g(0), b(0))
        zc, q, kp, vp, kb, vb, cst = _proj_in_prompt(xp, wi, conv_w, kp, vp, l, batch, seq)
        za = _attn_prompt(q, kb, vb, band, lam_p, subln, lam_init, batch, seq)
        xp = _mix_ffn_ln(xp, zc, za, wo, *f2, l, g(1), b(1), g(2), b(2))
        conv_p.append(cst)

        xs = _ffn_ln(xs, *f1, l, g(0), b(0))
        zc, q, ks, vs, cst = _proj_in_sample(xs, wi, conv_w, ks, vs, l, state_conv[l], dec_seq)
        per_seq = lambda a: a.reshape(depth * n_dec, dec_seq * N_HEADS, HEAD_W)
        za = _attn_sample(q.reshape(n_dec, dec_seq, ATTN_W), per_seq(ks), per_seq(vs), ck, cv, l,
                          page_table, pastb, newb, lam_p, subln, lam_init)
        xs = _mix_ffn_ln(xs, zc, za.reshape(n_dec * dec_seq, ATTN_W), wo, *f2, l, g(1), b(1), g(2), b(2))
        conv_s.append(cst)

    return (xp.reshape(batch, seq, D_MODEL), xs.reshape(n_dec, dec_seq, D_MODEL),
            kp.reshape(depth, batch, seq, N_HEADS, HEAD_W), vp.reshape(depth, batch, seq, N_HEADS, V_DIM),
            jnp.stack(conv_p),
            ks.reshape(depth, n_dec, dec_seq, N_HEADS, HEAD_W),
            vs.reshape(depth, n_dec, dec_seq, N_HEADS, V_DIM), jnp.stack(conv_s))
```

```python
import functools
import math

import jax
import jax.numpy as jnp
from jax import lax
from jax.experimental import pallas as pl
from jax.experimental.pallas import tpu as pltpu

F32 = jnp.float32
BF16 = jnp.bfloat16

D_MODEL = 1024
DEPTH = 2
CONV_CH = D_MODEL // 2
CONV_K = 3
N_HEADS = 4
QK_DIM = 64
V_DIM = 2 * QK_DIM
HEAD_W = 2 * QK_DIM
ATTN_W = N_HEADS * V_DIM
D_FF = 2816
NUM_BUCKETS = 32
MAX_DISTANCE = 128
LN_EPS = 1e-5
ATTN_SCALE = QK_DIM ** -0.5
LOG2E = math.log2(math.e)
LOGIT_SCALE = ATTN_SCALE * LOG2E
ALPHA = (2 * DEPTH) ** 0.25
MASKED = -1e30

VMEM_LIMIT_BYTES = 56 * 1024 * 1024
TOKEN_TILE = 512
PROJ_PASSES = 2
FF_CHUNK = 256
Q_TILE = 512
SUBLANES = 8


def _params(n_axes):
    return pltpu.CompilerParams(dimension_semantics=("arbitrary",) * n_axes,
                                vmem_limit_bytes=VMEM_LIMIT_BYTES)


def _resident(shape):
    return pl.BlockSpec(shape, lambda *_: (0,) * len(shape), pipeline_mode=pl.Buffered(1))


def _layer_resident(stacked, layer, block_rows=None, row_block=0):
    _, rows, cols = stacked.shape
    return pl.BlockSpec((None, block_rows or rows, cols), lambda *_: (layer, row_block, 0),
                        pipeline_mode=pl.Buffered(1))


def _layer_norm(x, g, b):
    mu = jnp.mean(x, axis=-1, keepdims=True)
    xc = x - mu
    var = jnp.mean(xc * xc, axis=-1, keepdims=True)
    return xc * lax.rsqrt(var + LN_EPS) * g + b


def _swiglu_ln(x, wg_ref, wu_ref, wd_ref, g_ref, b_ref, h_scr):
    xb = x.astype(BF16)
    for c in range(D_FF // FF_CHUNK):
        sl = slice(c * FF_CHUNK, (c + 1) * FF_CHUNK)
        g = jnp.dot(xb, wg_ref[:, sl], preferred_element_type=F32)
        u = jnp.dot(xb, wu_ref[:, sl], preferred_element_type=F32)
        h_scr[:, sl] = (g * jax.nn.sigmoid(g) * u).astype(BF16)
    y = jnp.dot(h_scr[...], wd_ref[...], preferred_element_type=F32)
    return _layer_norm(ALPHA * x + 0.5 * y, g_ref[...], b_ref[...])


def _ffn_ln_kernel(x_ref, wg_ref, wu_ref, wd_ref, g_ref, b_ref, o_ref, h_scr):
    o_ref[...] = _swiglu_ln(x_ref[...], wg_ref, wu_ref, wd_ref, g_ref, b_ref, h_scr)


def _ffn_ln(x, wg, wu, wd, layer, g, b):
    t = x.shape[0]
    tm = min(TOKEN_TILE, t)
    row = pl.BlockSpec((tm, D_MODEL), lambda i: (i, 0))
    return pl.pallas_call(
        _ffn_ln_kernel,
        grid=(t // tm,),
        in_specs=[row, _layer_resident(wg, layer), _layer_resident(wu, layer),
                  _layer_resident(wd, layer), _resident((1, D_MODEL)), _resident((1, D_MODEL))],
        out_specs=row,
        out_shape=jax.ShapeDtypeStruct((t, D_MODEL), F32),
        scratch_shapes=[pltpu.VMEM((tm, D_FF), BF16)],
        compiler_params=_params(1),
        name="ffn_ln",
    )(x, wg, wu, wd, g, b)


def _section(xb, w_ref, c):
    return jnp.dot(xb, w_ref[:, c * CONV_CH:(c + 1) * CONV_CH], preferred_element_type=F32)


def _store_head_rows(ref, r0, x):
    tokens = x.shape[0]
    for h in range(N_HEADS):
        ref[pl.ds(r0 * N_HEADS + h, tokens, stride=N_HEADS), :] = x[:, h * HEAD_W:(h + 1) * HEAD_W]


def _emit_qkv(xb, w_ref, r0, q_ref, k_ref, v_ref, kb_ref, vb_ref):
    rs = slice(r0, r0 + xb.shape[0])
    q_ref[rs, :] = (_section(xb, w_ref, 3) * LOGIT_SCALE).astype(q_ref.dtype)
    k = _section(xb, w_ref, 4)
    _store_head_rows(k_ref, r0, k)
    v = _section(xb, w_ref, 5)
    _store_head_rows(v_ref, r0, v)
    if kb_ref is not None:
        kb_ref[rs, :] = k.astype(BF16)
        vb_ref[rs, :] = v.astype(BF16)


def _proj_in_prompt_kernel(x_ref, w_ref, cw_ref, k_all_ref, v_all_ref, zc_ref, q_ref, k_ref, v_ref,
                           kb_ref, vb_ref, cs_ref, u_scr, *, tiles_per_seq):
    del k_all_ref, v_all_ref
    tm = x_ref.shape[0]
    hm = tm // PROJ_PASSES

    @pl.when(pl.program_id(0) % tiles_per_seq == 0)
    def _():
        u_scr[0:SUBLANES, :] = jnp.zeros((SUBLANES, CONV_CH), F32)

    cw = cw_ref[...]
    for r0 in range(0, tm, hm):
        rs = slice(r0, r0 + hm)
        xb = x_ref[rs, :].astype(BF16)
        bg = _section(xb, w_ref, 0)
        u = _section(xb, w_ref, 1) * _section(xb, w_ref, 2)
        u_scr[SUBLANES + r0:SUBLANES + r0 + hm, :] = u
        um1 = u_scr[SUBLANES - 1 + r0:SUBLANES - 1 + r0 + hm, :]
        um2 = u_scr[SUBLANES - 2 + r0:SUBLANES - 2 + r0 + hm, :]
        y = cw[0:1] * um2 + cw[1:2] * um1 + cw[2:3] * u
        zc_ref[rs, :] = (bg * y).astype(zc_ref.dtype)
        _emit_qkv(xb, w_ref, r0, q_ref, k_ref, v_ref, kb_ref, vb_ref)
    u_scr[0:SUBLANES, :] = u_scr[tm:tm + SUBLANES, :]
    cs_ref[0] = u[hm - (CONV_K - 1):hm, :]


def _head_rows_spec(tm, n_tiles, layer):
    return pl.BlockSpec((tm * N_HEADS, HEAD_W), lambda i: (layer * n_tiles + i, 0))


def _proj_in_prompt(x, w_in, conv_w, k_all, v_all, layer, batch, seq):
    t = x.shape[0]
    tm = TOKEN_TILE
    tiles_per_seq = seq // tm
    row = lambda w: pl.BlockSpec((tm, w), lambda i: (i, 0))
    hbm = pl.BlockSpec(memory_space=pl.ANY)
    head_rows = _head_rows_spec(tm, t // tm, layer)
    same = lambda a: jax.ShapeDtypeStruct(a.shape, a.dtype)
    return pl.pallas_call(
        functools.partial(_proj_in_prompt_kernel, tiles_per_seq=tiles_per_seq),
        grid=(t // tm,),
        in_specs=[row(D_MODEL), _layer_resident(w_in, layer), _layer_resident(conv_w, layer), hbm, hbm],
        out_specs=[row(CONV_CH), row(ATTN_W), head_rows, head_rows, row(ATTN_W), row(ATTN_W),
                   pl.BlockSpec((1, CONV_K - 1, CONV_CH), lambda i: (i // tiles_per_seq, 0, 0))],
        out_shape=[jax.ShapeDtypeStruct((t, CONV_CH), BF16),
                   jax.ShapeDtypeStruct((t, ATTN_W), BF16),
                   same(k_all), same(v_all),
                   jax.ShapeDtypeStruct((t, ATTN_W), BF16),
                   jax.ShapeDtypeStruct((t, ATTN_W), BF16),
                   jax.ShapeDtypeStruct((batch, CONV_K - 1, CONV_CH), F32)],
        input_output_aliases={3: 2, 4: 3},
        scratch_shapes=[pltpu.VMEM((tm + 2 * SUBLANES, CONV_CH), F32)],
        compiler_params=_params(1),
        name="proj_in_prompt",
    )(x, w_in, conv_w, k_all, v_all)


def _proj_in_sample_kernel(x_ref, w_ref, cw_ref, p1_ref, p2_ref, k_all_ref, v_all_ref, zc_ref, q_ref,
                           k_ref, v_ref, cs_ref, u_scr, *, dec_seq):
    del k_all_ref, v_all_ref
    tm = x_ref.shape[0]
    xb = x_ref[...].astype(BF16)
    bg = _section(xb, w_ref, 0)
    u = _section(xb, w_ref, 1) * _section(xb, w_ref, 2)
    u_scr[0:SUBLANES, :] = jnp.zeros((SUBLANES, CONV_CH), F32)
    u_scr[SUBLANES:SUBLANES + tm, :] = u
    tpos = lax.broadcasted_iota(jnp.int32, (tm, CONV_CH), 0) % dec_seq
    um1 = jnp.where(tpos >= 1, u_scr[SUBLANES - 1:SUBLANES - 1 + tm, :], p1_ref[...])
    um2 = jnp.where(tpos >= 2, u_scr[SUBLANES - 2:SUBLANES - 2 + tm, :], p2_ref[...])
    cw = cw_ref[...]
    y = cw[0:1] * um2 + cw[1:2] * um1 + cw[2:3] * u
    zc_ref[...] = (bg * y).astype(zc_ref.dtype)
    cs_ref[...] = u.reshape(tm // dec_seq, dec_seq, CONV_CH)[:, dec_seq - (CONV_K - 1):, :]
    _emit_qkv(xb, w_ref, 0, q_ref, k_ref, v_ref, None, None)


def _proj_in_sample(x, w_in, conv_w, k_all, v_all, layer, prev, dec_seq):
    t = x.shape[0]
    n_seq = t // dec_seq
    tm = min(TOKEN_TILE, t)
    p2 = jnp.pad(prev, ((0, 0), (0, dec_seq - (CONV_K - 1)), (0, 0))).reshape(t, CONV_CH)
    p1 = jnp.pad(prev[:, 1:], ((0, 0), (0, dec_seq - 1), (0, 0))).reshape(t, CONV_CH)
    row = lambda w: pl.BlockSpec((tm, w), lambda i: (i, 0))
    hbm = pl.BlockSpec(memory_space=pl.ANY)
    head_rows = _head_rows_spec(tm, t // tm, layer)
    same = lambda a: jax.ShapeDtypeStruct(a.shape, a.dtype)
    return pl.pallas_call(
        functools.partial(_proj_in_sample_kernel, dec_seq=dec_seq),
        grid=(t // tm,),
        in_specs=[row(D_MODEL), _layer_resident(w_in, layer), _layer_resident(conv_w, layer),
                  row(CONV_CH), row(CONV_CH), hbm, hbm],
        out_specs=[row(CONV_CH), row(ATTN_W), head_rows, head_rows,
                   pl.BlockSpec((tm // dec_seq, CONV_K - 1, CONV_CH), lambda i: (i, 0, 0))],
        out_shape=[jax.ShapeDtypeStruct((t, CONV_CH), BF16),
                   jax.ShapeDtypeStruct((t, ATTN_W), F32),
                   same(k_all), same(v_all),
                   jax.ShapeDtypeStruct((n_seq, CONV_K - 1, CONV_CH), F32)],
        input_output_aliases={5: 2, 6: 3},
        scratch_shapes=[pltpu.VMEM((tm + 2 * SUBLANES, CONV_CH), F32)],
        compiler_params=_params(1),
        name="proj_in_sample",
    )(x, w_in, conv_w, p1, p2, k_all, v_all)


def _bias_kernel(tab_ref, o_ref, *, rmod, offset, cmax):
    h = pl.program_id(0)
    shape = o_ref.shape[1:]
    r = lax.broadcasted_iota(jnp.int32, shape, 0)
    c = lax.broadcasted_iota(jnp.int32, shape, 1)
    dist = (r % rmod) - c + offset
    n = jnp.maximum(dist, 0)
    max_exact = NUM_BUCKETS // 2
    nf = jnp.maximum(n, 1).astype(F32)
    large = max_exact + (jnp.log(nf / max_exact) / math.log(MAX_DISTANCE / max_exact)
                         * (NUM_BUCKETS - max_exact)).astype(jnp.int32)
    large = jnp.minimum(large, NUM_BUCKETS - 1)
    bucket = jnp.where(n < max_exact, n, large)
    last = tab_ref[NUM_BUCKETS - 1, h]
    bias = jnp.zeros(shape, F32)
    for b in range(NUM_BUCKETS - 1):
        bias = jnp.where(bucket == b, (tab_ref[b, h] - last) * LOG2E, bias)
    o_ref[0] = jnp.where((dist >= 0) & (c < cmax), bias, MASKED)


def _bias_tiles(table, rows, cols, rmod, offset, cmax):
    return pl.pallas_call(
        functools.partial(_bias_kernel, rmod=rmod, offset=offset, cmax=cmax),
        grid=(N_HEADS,),
        in_specs=[pl.BlockSpec(memory_space=pltpu.SMEM)],
        out_specs=pl.BlockSpec((1, rows, cols), lambda h: (h, 0, 0)),
        out_shape=jax.ShapeDtypeStruct((N_HEADS, rows, cols), F32),
        compiler_params=_params(1),
        name="rel_bias_tiles",
    )(table)


def _lambda_full(lam_ref, lam_init):
    lp = lam_ref[...]
    d1 = jnp.sum(lp[0:1] * lp[1:2], axis=-1, keepdims=True)
    d2 = jnp.sum(lp[2:3] * lp[3:4], axis=-1, keepdims=True)
    return jnp.exp(d1) - jnp.exp(d2) + lam_init


def _head_out(o1, o2, lam, subln, lam_init):
    od = o1 - lam * o2
    ms = jnp.mean(od * od, axis=-1, keepdims=True)
    return od * lax.rsqrt(ms + LN_EPS) * subln * (1.0 - lam_init)


M_INIT = -1e29


def _attn_prompt_kernel(lam_ref, subln_ref, q_ref, k_ref, v_ref, diag_ref, corner_ref, o_ref,
                        qs_scr, s0, s1, p0, p1, a0, a1, m_scr, l_scr, acc_scr, *, lam_init):
    tq = Q_TILE
    near = MAX_DISTANCE
    rows = 2 * tq
    nq = q_ref.shape[0] // tq
    n_pairs = nq * (nq + 1) // 2
    assert (n_pairs + 2) % 2 == 0
    lane = lax.broadcasted_iota(jnp.int32, (tq, HEAD_W), 1)

    s0[...] = jnp.full(s0.shape, MASKED, F32)
    s1[...] = jnp.full(s1.shape, MASKED, F32)
    p0[...] = jnp.zeros(p0.shape, BF16)
    p1[...] = jnp.zeros(p1.shape, BF16)
    a0[...] = jnp.ones(a0.shape, F32)
    a1[...] = jnp.ones(a1.shape, F32)
    m_scr[...] = jnp.full(m_scr.shape, M_INIT, F32)
    l_scr[...] = jnp.zeros(l_scr.shape, F32)
    acc_scr[...] = jnp.zeros(acc_scr.shape, F32)
    qs_scr[...] = jnp.zeros(qs_scr.shape, BF16)

    def key_rows(j):
        return pl.ds(pl.multiple_of(j * tq, tq), tq)

    def half_step(pair_a, pair_b, pair_c, s_a, s_b, p_b, a_b, p_c, a_c):
        ia, ja = pair_a
        ib, jb = pair_b
        ic, jc = pair_c

        @pl.when((ja == 0) | (jb == 0) | (jc == 0) | (jb >= ib - 1) | (ib >= nq))
        def _():
            @pl.when((ja == 0) & (ia < nq))
            def _():
                q = q_ref[pl.ds(pl.multiple_of(ia * tq, tq), tq), :]
                zero = jnp.zeros_like(q)
                qs_scr[ia % 2, 0:tq, :] = jnp.where(lane < QK_DIM, q, zero)
                qs_scr[ia % 2, tq:rows, :] = jnp.where(lane >= QK_DIM, q, zero)

            @pl.when(ib >= nq)
            def _():
                s_b[...] = jnp.full(s_b.shape, MASKED, F32)

            @pl.when((ib < nq) & (jb == ib))
            def _():
                s_b[0:tq, :] = s_b[0:tq, :] + diag_ref[0]
                s_b[tq:rows, :] = s_b[tq:rows, :] + diag_ref[0]

            @pl.when((ib < nq) & (jb == ib - 1))
            def _():
                for r0 in (0, tq):
                    corner = (slice(r0, r0 + near), slice(tq - near, tq))
                    s_b[corner] = s_b[corner] + corner_ref[0]

            @pl.when(jb == 0)
            def _():
                m_scr[ib % 2] = jnp.full(m_scr.shape[1:], M_INIT, F32)
                l_scr[ib % 2] = jnp.zeros(l_scr.shape[1:], F32)

            @pl.when(jc == 0)
            def _():
                acc_scr[...] = jnp.zeros(acc_scr.shape, F32)

        acc_scr[...] = a_c[...] * acc_scr[...] + jnp.dot(p_c[...], v_ref[key_rows(jc), :],
                                                         preferred_element_type=F32)
        s_a[...] = lax.dot_general(qs_scr[ia % 2], k_ref[key_rows(ja), :], (((1,), (1,)), ((), ())),
                                   preferred_element_type=F32)
        par = ib % 2
        s = s_b[...]
        m_prev = m_scr[par]
        m_new = jnp.maximum(m_prev, jnp.max(s, axis=-1, keepdims=True))
        alpha = jnp.exp2(m_prev - m_new)
        p = jnp.exp2(s - jnp.concatenate([m_new] * (tq // HEAD_W), axis=1))
        l_scr[par] = alpha * l_scr[par] + jnp.sum(p, axis=-1, keepdims=True)
        m_scr[par] = m_new
        p_b[...] = p.astype(BF16)
        a_b[...] = alpha

        @pl.when((jc == ic) & (ic < nq))
        def _():
            o = acc_scr[...] / l_scr[ic % 2]
            z = _head_out(o[0:tq], o[tq:rows], _lambda_full(lam_ref, lam_init), subln_ref[...],
                          lam_init)
            o_ref[pl.ds(pl.multiple_of(ic * tq, tq), tq), :] = z.astype(o_ref.dtype)

    def advance(pair):
        i, j = pair
        wrap = j >= i
        return jnp.where(wrap, i + 1, i), jnp.where(wrap, 0, j + 1)

    def body(_, carry):
        pair_a, pair_b, pair_c = carry
        half_step(pair_a, pair_b, pair_c, s0, s1, p1, a1, p0, a0)
        pair_a, pair_b, pair_c = advance(pair_a), pair_a, pair_b
        half_step(pair_a, pair_b, pair_c, s1, s0, p0, a0, p1, a1)
        return advance(pair_a), pair_a, pair_b

    idle = (jnp.int32(0), jnp.int32(1))
    lax.fori_loop(0, (n_pairs + 2) // 2, body, ((jnp.int32(0), jnp.int32(0)), idle, idle))


def _attn_prompt(q, kb, vb, diag, corner, lam_p, subln, lam_init, batch, seq):
    t = q.shape[0]
    tq = Q_TILE
    rows = 2 * tq
    seq_spec = pl.BlockSpec((seq, HEAD_W), lambda b, h: (b, h))
    per_head = lambda a: pl.BlockSpec((1,) + a.shape[1:], lambda b, h: (h, 0, 0))
    return pl.pallas_call(
        functools.partial(_attn_prompt_kernel, lam_init=lam_init),
        grid=(batch, N_HEADS),
        in_specs=[pl.BlockSpec((4, QK_DIM), lambda b, h: (0, 0)),
                  pl.BlockSpec((1, V_DIM), lambda b, h: (0, 0)),
                  seq_spec, seq_spec, seq_spec, per_head(diag), per_head(corner)],
        out_specs=seq_spec,
        out_shape=jax.ShapeDtypeStruct((t, ATTN_W), BF16),
        scratch_shapes=[pltpu.VMEM((2, rows, HEAD_W), BF16),
                        pltpu.VMEM((rows, tq), F32), pltpu.VMEM((rows, tq), F32),
                        pltpu.VMEM((rows, tq), BF16), pltpu.VMEM((rows, tq), BF16),
                        pltpu.VMEM((rows, V_DIM), F32), pltpu.VMEM((rows, V_DIM), F32),
                        pltpu.VMEM((2, rows, V_DIM), F32), pltpu.VMEM((2, rows, V_DIM), F32),
                        pltpu.VMEM((rows, V_DIM), F32)],
        compiler_params=_params(2),
        name="attn_prompt",
    )(lam_p, subln, q, kb, vb, diag, corner)


def _page_copies(pt_ref, ck_ref, cv_ref, kbuf, vbuf, sem, seq, slot, layer, n_pages):
    rows = ck_ref.shape[2]
    copies = []
    for p in range(n_pages):
        phys = 0 if seq is None else pt_ref[seq * n_pages + p]
        dst = pl.ds(p * rows, rows)
        copies.append(pltpu.make_async_copy(ck_ref.at[layer, phys], kbuf.at[slot, dst], sem.at[slot, 0]))
        copies.append(pltpu.make_async_copy(cv_ref.at[layer, phys], vbuf.at[slot, dst], sem.at[slot, 1]))
    return copies


def _attn_sample_kernel(pt_ref, lam_ref, subln_ref, q_ref, kn_ref, vn_ref, pastb_ref, newb_ref,
                        ck_ref, cv_ref, o_ref, kbuf, vbuf, kn_scr, vn_scr, sem,
                        *, lam_init, layer, n_pages):
    seq = pl.program_id(0)
    slot = seq % 2
    copies = functools.partial(_page_copies, pt_ref, ck_ref, cv_ref, kbuf, vbuf, sem,
                               layer=layer, n_pages=n_pages)

    @pl.when(seq == 0)
    def _():
        for c in copies(seq=0, slot=0):
            c.start()

    @pl.when(seq + 1 < pl.num_programs(0))
    def _():
        for c in copies(seq=seq + 1, slot=1 - slot):
            c.start()

    dec_seq = q_ref.shape[1]
    q = q_ref[0]
    lane = lax.broadcasted_iota(jnp.int32, (dec_seq, HEAD_W), 1)
    blocks = []
    for h in range(N_HEADS):
        qh = q[:, h * HEAD_W:(h + 1) * HEAD_W]
        blocks += [jnp.where(lane < QK_DIM, qh, 0.0), jnp.where(lane >= QK_DIM, qh, 0.0)]
    qall = jnp.concatenate(blocks, axis=0).astype(BF16)

    def logits(k):
        return lax.dot_general(qall, k, (((1,), (1,)), ((), ())), preferred_element_type=F32)

    n_new = kn_ref.shape[1]
    kn_scr[...] = jnp.zeros(kn_scr.shape, F32)
    vn_scr[...] = jnp.zeros(vn_scr.shape, F32)
    kn_scr[0:n_new, :] = kn_ref[0]
    vn_scr[0:n_new, :] = vn_ref[0]
    s_new = logits(kn_scr[...].astype(BF16)) + newb_ref[...]

    for c in copies(seq=None, slot=slot):
        c.wait()
    s_past = logits(kbuf[slot].astype(BF16)) + pastb_ref[...]
    m = jnp.maximum(jnp.max(s_past, axis=-1, keepdims=True), jnp.max(s_new, axis=-1, keepdims=True))
    p_past = jnp.exp2(s_past - m)
    p_new = jnp.exp2(s_new - m)
    l = jnp.sum(p_past, axis=-1, keepdims=True) + jnp.sum(p_new, axis=-1, keepdims=True)
    acc = (jnp.dot(p_past.astype(BF16), vbuf[slot].astype(BF16), preferred_element_type=F32)
           + jnp.dot(p_new.astype(BF16), vn_scr[...].astype(BF16), preferred_element_type=F32))
    o = acc / l
    lam = _lambda_full(lam_ref, lam_init)
    heads = []
    for h in range(N_HEADS):
        oh = o[2 * dec_seq * h:2 * dec_seq * (h + 1)]
        heads.append(_head_out(oh[0:dec_seq], oh[dec_seq:2 * dec_seq], lam, subln_ref[...], lam_init))
    o_ref[0] = jnp.concatenate(heads, axis=-1)


def _attn_sample(q, k_new, v_new, cache_k, cache_v, layer, page_table, pastb, newb, lam_p, subln,
                 lam_init):
    n_seq, dec_seq, _ = q.shape
    n_pages = page_table.shape[1]
    page_rows = cache_k.shape[2]
    rows = N_HEADS * 2 * dec_seq
    past_rows = n_pages * page_rows
    per_seq = lambda a: pl.BlockSpec((1,) + a.shape[1:], lambda s, pt: (s, 0, 0))
    new_rows = pl.BlockSpec((1,) + k_new.shape[1:], lambda s, pt: (layer * n_seq + s, 0, 0))
    const = lambda shape: pl.BlockSpec(shape, lambda s, pt: (0,) * len(shape),
                                       pipeline_mode=pl.Buffered(1))
    hbm = pl.BlockSpec(memory_space=pl.ANY)
    grid_spec = pltpu.PrefetchScalarGridSpec(
        num_scalar_prefetch=1,
        grid=(n_seq,),
        in_specs=[const((4, QK_DIM)), const((1, V_DIM)), per_seq(q), new_rows, new_rows,
                  const((rows, past_rows)), const((rows, HEAD_W)), hbm, hbm],
        out_specs=per_seq(q),
        scratch_shapes=[pltpu.VMEM((2, past_rows, HEAD_W), F32), pltpu.VMEM((2, past_rows, HEAD_W), F32),
                        pltpu.VMEM((HEAD_W, HEAD_W), F32), pltpu.VMEM((HEAD_W, HEAD_W), F32),
                        pltpu.SemaphoreType.DMA((2, 2))],
    )
    return pl.pallas_call(
        functools.partial(_attn_sample_kernel, lam_init=lam_init, layer=layer, n_pages=n_pages),
        grid_spec=grid_spec,
        out_shape=jax.ShapeDtypeStruct((n_seq, dec_seq, ATTN_W), F32),
        compiler_params=_params(1),
        name="attn_sample",
    )(page_table.reshape(-1), lam_p, subln, q, k_new, v_new, pastb, newb, cache_k, cache_v)


def _sample_bias_tiles(table, dec_seq, page, n_pages):
    rows = N_HEADS * 2 * dec_seq
    lastb = _bias_tiles(table, 2 * dec_seq, page, dec_seq, page, page).reshape(rows, page)
    newb = _bias_tiles(table, 2 * dec_seq, page, dec_seq, 0, dec_seq).reshape(rows, page)
    row_head = jnp.arange(rows)[:, None] // (2 * dec_seq)
    own = lambda n_tok: (jnp.arange(n_tok * N_HEADS)[None, :] % N_HEADS) == row_head
    spread = lambda a: jnp.repeat(a, N_HEADS, axis=1)
    generic = jnp.where(own(page), 0.0, MASKED).astype(F32)
    last = jnp.where(own(page), spread(lastb), MASKED)
    pastb = jnp.concatenate([jnp.tile(generic, (1, n_pages - 1)), last], axis=1)
    new = jnp.where(own(dec_seq), spread(newb[:, :dec_seq]), MASKED)
    newb_full = jnp.concatenate(
        [new, jnp.full((rows, HEAD_W - dec_seq * N_HEADS), MASKED, F32)], axis=1)
    return pastb, newb_full


def _mix_ffn_ln_kernel(x_ref, zc_ref, za_ref, wc_ref, wa_ref, g1_ref, b1_ref,
                       wg_ref, wu_ref, wd_ref, g2_ref, b2_ref, o_ref, h_scr):
    mix = (jnp.dot(zc_ref[...].astype(BF16), wc_ref[...], preferred_element_type=F32)
           + jnp.dot(za_ref[...].astype(BF16), wa_ref[...], preferred_element_type=F32))
    x = _layer_norm(ALPHA * x_ref[...] + mix, g1_ref[...], b1_ref[...])
    o_ref[...] = _swiglu_ln(x, wg_ref, wu_ref, wd_ref, g2_ref, b2_ref, h_scr)


def _mix_ffn_ln(x, zc, za, w_out, wg, wu, wd, layer, g1, b1, g2, b2):
    t = x.shape[0]
    tm = min(TOKEN_TILE, t)
    row = lambda w: pl.BlockSpec((tm, w), lambda i: (i, 0))
    vec = _resident((1, D_MODEL))
    return pl.pallas_call(
        _mix_ffn_ln_kernel,
        grid=(t // tm,),
        in_specs=[row(D_MODEL), row(CONV_CH), row(ATTN_W), _layer_resident(w_out, layer, CONV_CH, 0),
                  _layer_resident(w_out, layer, ATTN_W, CONV_CH // ATTN_W), vec, vec,
                  _layer_resident(wg, layer), _layer_resident(wu, layer), _layer_resident(wd, layer),
                  vec, vec],
        out_specs=row(D_MODEL),
        out_shape=jax.ShapeDtypeStruct((t, D_MODEL), F32),
        scratch_shapes=[pltpu.VMEM((tm, D_FF), BF16)],
        compiler_params=_params(1),
        name="mix_ffn_ln",
    )(x, zc, za, w_out, w_out, g1, b1, wg, wu, wd, g2, b2)


def kernel(x_prompt, x_sample, cache_k, cache_v, state_conv, page_table, rel_bias_table,
           w_in, w_out, conv_w, lambda_q1, lambda_k1, lambda_q2, lambda_k2, subln_w,
           ln_g, ln_b, ffn1_w_gate, ffn1_w_up, ffn1_w_down,
           ffn2_w_gate, ffn2_w_up, ffn2_w_down):
    batch, seq, _ = x_prompt.shape
    n_dec, dec_seq, _ = x_sample.shape
    depth, n_phys, page = cache_k.shape[:3]

    xp = x_prompt.reshape(batch * seq, D_MODEL)
    xs = x_sample.reshape(n_dec * dec_seq, D_MODEL)
    ck = cache_k.reshape(depth, n_phys, page * N_HEADS, HEAD_W)
    cv = cache_v.reshape(depth, n_phys, page * N_HEADS, HEAD_W)
    bf = lambda w: w.astype(BF16)
    f1 = (bf(ffn1_w_gate), bf(ffn1_w_up), bf(ffn1_w_down))
    f2 = (bf(ffn2_w_gate), bf(ffn2_w_up), bf(ffn2_w_down))
    wi, wo = bf(w_in), bf(w_out)

    diag = _bias_tiles(rel_bias_table, Q_TILE, Q_TILE, Q_TILE, 0, Q_TILE)
    corner = _bias_tiles(rel_bias_table, MAX_DISTANCE, MAX_DISTANCE, MAX_DISTANCE, MAX_DISTANCE,
                         MAX_DISTANCE)
    pastb, newb = _sample_bias_tiles(rel_bias_table, dec_seq, page, page_table.shape[1])

    head_rows = lambda tokens: jnp.zeros((depth * tokens * N_HEADS, HEAD_W), F32)
    kp, vp = head_rows(batch * seq), head_rows(batch * seq)
    ks, vs = head_rows(n_dec * dec_seq), head_rows(n_dec * dec_seq)
    conv_p, conv_s = [], []
    for l in range(depth):
        lam_init = 0.8 - 0.6 * math.exp(-0.3 * l)
        lam_p = jnp.stack([lambda_q1[l], lambda_k1[l], lambda_q2[l], lambda_k2[l]])
        subln = subln_w[l].reshape(1, V_DIM)
        g = lambda j: ln_g[l, j].reshape(1, D_MODEL)
        b = lambda j: ln_b[l, j].reshape(1, D_MODEL)

        xp = _ffn_ln(xp, *f1, l, g(0), b(0))
        zc, q, kp, vp, kb, vb, cst = _proj_in_prompt(xp, wi, conv_w, kp, vp, l, batch, seq)
        za = _attn_prompt(q, kb, vb, diag, corner, lam_p, subln, lam_init, batch, seq)
        xp = _mix_ffn_ln(xp, zc, za, wo, *f2, l, g(1), b(1), g(2), b(2))
        conv_p.append(cst)

        xs = _ffn_ln(xs, *f1, l, g(0), b(0))
        zc, q, ks, vs, cst = _proj_in_sample(xs, wi, conv_w, ks, vs, l, state_conv[l], dec_seq)
        per_seq = lambda a: a.reshape(depth * n_dec, dec_seq * N_HEADS, HEAD_W)
        za = _attn_sample(q.reshape(n_dec, dec_seq, ATTN_W), per_seq(ks), per_seq(vs), ck, cv, l,
                          page_table, pastb, newb, lam_p, subln, lam_init)
        xs = _mix_ffn_ln(xs, zc, za.reshape(n_dec * dec_seq, ATTN_W), wo, *f2, l, g(1), b(1), g(2), b(2))
        conv_s.append(cst)

    return (xp.reshape(batch, seq, D_MODEL), xs.reshape(n_dec, dec_seq, D_MODEL),
            kp.reshape(depth, batch, seq, N_HEADS, HEAD_W), vp.reshape(depth, batch, seq, N_HEADS, V_DIM),
            jnp.stack(conv_p),
            ks.reshape(depth, n_dec, dec_seq, N_HEADS, HEAD_W),
            vs.reshape(depth, n_dec, dec_seq, N_HEADS, V_DIM), jnp.stack(conv_s))
```
